```python
import math
import jax, jax.numpy as jnp
from jax import lax
import numpy as np

D_MODEL = 2048
BATCH = 2
SEQ = 4096
DEPTH = 1

DN_HEADS = 8
DN_DK = 128
DN_DV = 128
DN_CONV = 4
DN_CHUNK = 64
RT_HEADS = 8
RT_DK = 64
RT_DV = 128
RT_CHUNK = 64
ROPE_BASE = 10000.0
DN_WIDTH = DN_HEADS * DN_DV
RT_WIDTH = RT_HEADS * RT_DV
MIX_WIDTH = DN_WIDTH + RT_WIDTH
DN_CONV_CH = 2 * DN_HEADS * DN_DK + DN_WIDTH
IN_SPLITS = (DN_HEADS * DN_DK, DN_HEADS * DN_DK, DN_WIDTH, DN_WIDTH, DN_HEADS, DN_HEADS,
             RT_HEADS * RT_DK, RT_HEADS * RT_DK, RT_WIDTH, RT_WIDTH)
IN_WIDTH = sum(IN_SPLITS)
N_EXPERTS = 64
TOP_K = 8
N_GROUPS = 8
TOPK_GROUPS = 4
EXPERT_FF = 512
SHARED_FF = 512
ROUTED_SCALE = 2.5
MOE_BLOCK = 128
NORM_EPS = 1e-6

kernel_name = "hybrid_deltanet_retention_moe_layer"


def rms_norm(x, w):
    xf = x.astype(jnp.float32)
    y = xf * lax.rsqrt(jnp.mean(xf * xf, axis=-1, keepdims=True) + NORM_EPS)
    return (y * w.astype(jnp.float32)).astype(x.dtype)


def l2_normalize(x):
    xf = x.astype(jnp.float32)
    return xf * lax.rsqrt(jnp.sum(xf * xf, axis=-1, keepdims=True) + NORM_EPS)


def head_group_norm(o, w):
    mu = jnp.mean(o, axis=-1, keepdims=True)
    var = jnp.mean(jnp.square(o - mu), axis=-1, keepdims=True)
    y = (o - mu) * lax.rsqrt(var + NORM_EPS)
    b, s, h, d = o.shape
    return y.reshape(b, s, h * d) * w.astype(jnp.float32)


def swiglu(x, w_gate, w_up, w_down):
    return (jax.nn.silu(x @ w_gate) * (x @ w_up)) @ w_down


def to_chunks(t, chunk):
    b, s = t.shape[:2]
    t = t.reshape((b, s // chunk, chunk) + t.shape[2:])
    return jnp.moveaxis(t, 2, 3)


def from_chunks(o):
    b, n, h, c, d = o.shape
    return jnp.moveaxis(o, 3, 2).reshape(b, n * c, h, d)


def causal_depthwise_conv(x, w):
    k, ch = w.shape
    return lax.conv_general_dilated(x, w[:, None, :].astype(x.dtype), window_strides=(1,),
                                    padding=[(k - 1, 0)], dimension_numbers=('NWC', 'WIO', 'NWC'),
                                    feature_group_count=ch)


def rotary(x, positions):
    half = x.shape[-1] // 2
    theta = 1.0 / (ROPE_BASE ** jnp.linspace(0.0, 1.0, half, dtype=jnp.float32))
    ang = positions.astype(jnp.float32)[..., None] * theta
    cos, sin = jnp.cos(ang)[:, :, None, :], jnp.sin(ang)[:, :, None, :]
    x1, x2 = x[..., :half], x[..., half:]
    return jnp.concatenate([x1 * cos - x2 * sin, x2 * cos + x1 * sin], axis=-1)


def gated_delta_rule_chunked(q, k, v, g, beta):
    b, s, h, dk = q.shape
    dv = v.shape[-1]
    c = DN_CHUNK
    qc = to_chunks(q * (dk ** -0.5), c)
    kc = to_chunks(k, c)
    vc = to_chunks(v, c)
    gc = jnp.cumsum(to_chunks(g, c), axis=-1)
    bc = to_chunks(beta, c)
    causal = jnp.tril(jnp.ones((c, c), dtype=bool))
    strict = jnp.tril(jnp.ones((c, c), dtype=bool), -1)
    diff = gc[..., :, None] - gc[..., None, :]
    decay = jnp.where(causal, jnp.exp(jnp.where(causal, diff, 0.0)), 0.0)
    kb = kc * bc[..., None]
    a_mat = jnp.where(strict, jnp.einsum('bnhcd,bnhsd->bnhcs', kb, kc) * decay, 0.0)
    t_mat = a_mat + jnp.eye(c, dtype=a_mat.dtype)
    rhs = jnp.concatenate([kb * jnp.exp(gc)[..., None], vc * bc[..., None]], axis=-1)
    sol = lax.linalg.triangular_solve(t_mat, rhs, left_side=True, lower=True, unit_diagonal=True)
    w_c, u_c = sol[..., :dk], sol[..., dk:]
    attn = jnp.where(causal, jnp.einsum('bnhcd,bnhsd->bnhcs', qc, kc) * decay, 0.0)
    g_last = gc[..., -1:]
    q_dec = qc * jnp.exp(gc)[..., None]
    k_dec = kc * jnp.exp(g_last - gc)[..., None]
    chunk_decay = jnp.exp(g_last[..., 0])
    xs = tuple(jnp.moveaxis(t, 1, 0) for t in (q_dec, k_dec, w_c, u_c, attn, chunk_decay))

    def step(state, inp):
        qd, kd, wc, uc, at, cd = inp
        v_new = uc - jnp.einsum('bhcd,bhde->bhce', wc, state)
        out = jnp.einsum('bhcd,bhde->bhce', qd, state) + jnp.einsum('bhcs,bhse->bhce', at, v_new)
        state = state * cd[..., None, None] + jnp.einsum('bhcd,bhce->bhde', kd, v_new)
        return state, out

    s0 = jnp.zeros((b, h, dk, dv), jnp.float32)
    _, o = lax.scan(step, s0, xs)
    return from_chunks(jnp.moveaxis(o, 0, 1))


def retention_chunked(q, k, v, log_gamma):
    c = RT_CHUNK
    qc, kc, vc = to_chunks(q, c), to_chunks(k, c), to_chunks(v, c)
    idx = jnp.arange(c, dtype=jnp.float32)
    causal = jnp.tril(jnp.ones((c, c), dtype=bool))
    rel = jnp.where(causal, idx[:, None] - idx[None, :], 0.0)
    d_intra = jnp.where(causal, jnp.exp(rel[None] * log_gamma[:, None, None]), 0.0)
    intra = jnp.einsum('bnhcs,bnhse->bnhce', jnp.einsum('bnhcd,bnhsd->bnhcs', qc, kc) * d_intra, vc)
    q_dec = qc * jnp.exp((idx + 1.0)[None, :] * log_gamma[:, None])[..., None]
    k_dec = kc * jnp.exp((c - 1.0 - idx)[None, :] * log_gamma[:, None])[..., None]
    chunk_decay = jnp.exp(c * log_gamma)
    xs = tuple(jnp.moveaxis(t, 1, 0) for t in (q_dec, k_dec, vc))

    def step(state, inp):
        qd, kd, vv = inp
        out = jnp.einsum('bhcd,bhde->bhce', qd, state)
        state = state * chunk_decay[:, None, None] + jnp.einsum('bhcd,bhce->bhde', kd, vv)
        return state, out

    b, h, dk, dv = q.shape[0], q.shape[2], q.shape[3], v.shape[3]
    _, inter = lax.scan(step, jnp.zeros((b, h, dk, dv), jnp.float32), xs)
    return from_chunks(intra + jnp.moveaxis(inter, 0, 1))


def hybrid_mixer(h, positions, log_gamma, w_in, conv_w, a_log, dt_bias, dn_norm_w, rt_norm_w, w_out):
    b, s, _ = h.shape
    f32 = jnp.float32
    proj = h @ w_in
    split_at = np.cumsum(IN_SPLITS)[:-1].tolist()
    dq, dk_, dv_, dz, da, db, rq, rk, rv, rg = jnp.split(proj, split_at, axis=-1)
    qkv = jax.nn.silu(causal_depthwise_conv(jnp.concatenate([dq, dk_, dv_], axis=-1), conv_w))
    dq, dk_, dv_ = jnp.split(qkv, [DN_HEADS * DN_DK, 2 * DN_HEADS * DN_DK], axis=-1)
    q_a = l2_normalize(dq.reshape(b, s, DN_HEADS, DN_DK))
    k_a = l2_normalize(dk_.reshape(b, s, DN_HEADS, DN_DK))
    v_a = dv_.reshape(b, s, DN_HEADS, DN_DV).astype(f32)
    beta = jax.nn.sigmoid(db.astype(f32))
    g = -jnp.exp(a_log.astype(f32)) * jax.nn.softplus(da.astype(f32) + dt_bias.astype(f32))
    o_a = gated_delta_rule_chunked(q_a, k_a, v_a, g, beta)
    o_a = rms_norm(o_a, dn_norm_w).reshape(b, s, DN_WIDTH) * jax.nn.silu(dz.astype(f32))
    q_b = rotary(rq.reshape(b, s, RT_HEADS, RT_DK).astype(f32), positions)
    k_b = rotary(rk.reshape(b, s, RT_HEADS, RT_DK).astype(f32), positions) * (RT_DK ** -0.5)
    v_b = rv.reshape(b, s, RT_HEADS, RT_DV).astype(f32)
    o_b = retention_chunked(q_b, k_b, v_b, log_gamma)
    o_b = head_group_norm(o_b, rt_norm_w) * jax.nn.silu(rg.astype(f32))
    o = jnp.concatenate([o_a, o_b], axis=-1).astype(h.dtype)
    return o @ w_out


def moe_ffn(h, w_router, router_bias, w_gate_exp, w_up_exp, w_down_exp, w_gate_sh, w_up_sh, w_down_sh):
    b, s, d = h.shape
    n_tok = b * s
    hf = h.reshape(n_tok, d)
    f32 = jnp.float32
    scores = jax.nn.sigmoid(hf.astype(f32) @ w_router.astype(f32))
    sel = scores + router_bias.astype(f32)
    grp = sel.reshape(n_tok, N_GROUPS, N_EXPERTS // N_GROUPS)
    grp_score = jnp.sum(lax.top_k(grp, 2)[0], axis=-1)
    _, g_idx = lax.top_k(grp_score, TOPK_GROUPS)
    g_keep = jnp.any(g_idx[:, :, None] == jnp.arange(N_GROUPS)[None, None, :], axis=1)
    sel = jnp.where(jnp.repeat(g_keep, N_EXPERTS // N_GROUPS, axis=1), sel, -jnp.inf)
    _, e_idx = lax.top_k(sel, TOP_K)
    wts = jnp.take_along_axis(scores, e_idx, axis=1)
    wts = wts / jnp.sum(wts, axis=-1, keepdims=True) * ROUTED_SCALE
    n_assign = n_tok * TOP_K
    flat_e = e_idx.reshape(n_assign)
    flat_tok = jnp.repeat(jnp.arange(n_tok, dtype=jnp.int32), TOP_K)
    flat_w = wts.reshape(n_assign)
    order = jnp.argsort(flat_e)
    se, stok, sw = flat_e[order], flat_tok[order], flat_w[order]
    counts = jnp.zeros((N_EXPERTS,), jnp.int32).at[flat_e].add(1)
    starts = jnp.cumsum(counts) - counts
    padded = (counts + MOE_BLOCK - 1) // MOE_BLOCK * MOE_BLOCK
    pad_end = jnp.cumsum(padded)
    pad_start = pad_end - padded
    dest = pad_start[se] + (jnp.arange(n_assign, dtype=jnp.int32) - starts[se])
    n_blocks = -(-n_assign // MOE_BLOCK) + N_EXPERTS
    n_rows = n_blocks * MOE_BLOCK
    row_tok = jnp.zeros((n_rows,), jnp.int32).at[dest].set(stok)
    row_w = jnp.zeros((n_rows,), f32).at[dest].set(sw)
    block_start = jnp.arange(n_blocks, dtype=jnp.int32) * MOE_BLOCK
    block_e = jnp.minimum(jnp.searchsorted(pad_end, block_start, side='right'), N_EXPERTS - 1)

    def expert_block(args):
        tok, wt, e = args
        y = swiglu(hf[tok], w_gate_exp[e], w_up_exp[e], w_down_exp[e])
        return y * wt[:, None].astype(y.dtype)

    ys = lax.map(expert_block, (row_tok.reshape(n_blocks, MOE_BLOCK), row_w.reshape(n_blocks, MOE_BLOCK), block_e))
    routed = jax.ops.segment_sum(ys.reshape(n_rows, d), row_tok, num_segments=n_tok)
    shared = swiglu(hf, w_gate_sh, w_up_sh, w_down_sh)
    return (routed + shared).reshape(b, s, d)


def setup_inputs(seed: int = 0) -> dict:
    key = jax.random.key(seed)
    ks = jax.random.split(key, 24)
    f32 = jnp.float32
    L, D = DEPTH, D_MODEL

    def nrm(k, shape, scale):
        return jax.random.normal(k, shape, f32) * scale

    def gain(k, shape):
        return 1.0 + 0.05 * jax.random.normal(k, shape, f32)

    x = nrm(ks[0], (BATCH, SEQ, D), 1.0)
    c = nrm(ks[1], (BATCH, D), 1.0)
    positions = (jnp.arange(SEQ, dtype=jnp.int32)[None, :]
                 + jax.random.randint(ks[2], (BATCH, 1), 0, 1024, dtype=jnp.int32))
    w_ada = nrm(ks[3], (L, D, 6 * D), 0.5 * D ** -0.5)
    b_ada = nrm(ks[4], (L, 6 * D), 0.02)
    pre_norm_mix = gain(ks[5], (L, D))
    post_norm_mix = gain(ks[6], (L, D))
    w_in = nrm(ks[7], (L, D, IN_WIDTH), D ** -0.5)
    conv_w = nrm(ks[8], (L, DN_CONV, DN_CONV_CH), DN_CONV ** -0.5)
    a_log = jnp.log(jax.random.uniform(ks[9], (L, DN_HEADS), f32, 1.0, 16.0))
    dt = jnp.exp(jax.random.uniform(ks[10], (L, DN_HEADS), f32, math.log(1e-3), math.log(1e-1)))
    dt_bias = dt + jnp.log(-jnp.expm1(-dt))
    dn_norm_w = gain(ks[11], (L, DN_DV))
    rt_norm_w = gain(ks[12], (L, RT_WIDTH))
    w_out = nrm(ks[13], (L, MIX_WIDTH, D), MIX_WIDTH ** -0.5)
    pre_norm_ffn = gain(ks[14], (L, D))
    post_norm_ffn = gain(ks[15], (L, D))
    w_router = nrm(ks[16], (L, D, N_EXPERTS), D ** -0.5)
    router_bias = nrm(ks[17], (L, N_EXPERTS), 0.01)
    w_gate_exp = nrm(ks[18], (L, N_EXPERTS, D, EXPERT_FF), D ** -0.5)
    w_up_exp = nrm(ks[19], (L, N_EXPERTS, D, EXPERT_FF), D ** -0.5)
    w_down_exp = nrm(ks[20], (L, N_EXPERTS, EXPERT_FF, D), EXPERT_FF ** -0.5)
    w_gate_sh = nrm(ks[21], (L, D, SHARED_FF), D ** -0.5)
    w_up_sh = nrm(ks[22], (L, D, SHARED_FF), D ** -0.5)
    w_down_sh = nrm(ks[23], (L, SHARED_FF, D), SHARED_FF ** -0.5)
    return {"x": x, "c": c, "positions": positions, "w_ada": w_ada, "b_ada": b_ada,
            "pre_norm_mix": pre_norm_mix, "post_norm_mix": post_norm_mix, "w_in": w_in,
            "conv_w": conv_w, "a_log": a_log, "dt_bias": dt_bias, "dn_norm_w": dn_norm_w,
            "rt_norm_w": rt_norm_w, "w_out": w_out, "pre_norm_ffn": pre_norm_ffn,
            "post_norm_ffn": post_norm_ffn, "w_router": w_router, "router_bias": router_bias,
            "w_gate_exp": w_gate_exp, "w_up_exp": w_up_exp, "w_down_exp": w_down_exp,
            "w_gate_sh": w_gate_sh, "w_up_sh": w_up_sh, "w_down_sh": w_down_sh}


def reference(x, c, positions, w_ada, b_ada, pre_norm_mix, post_norm_mix, w_in, conv_w, a_log,
              dt_bias, dn_norm_w, rt_norm_w, w_out, pre_norm_ffn, post_norm_ffn, w_router,
              router_bias, w_gate_exp, w_up_exp, w_down_exp, w_gate_sh, w_up_sh, w_down_sh):
    log_gamma = jnp.log(1.0 - 2.0 ** (-5.0 - jnp.arange(RT_HEADS, dtype=jnp.float32)))
    c_act = jax.nn.silu(c)
    for l in range(DEPTH):
        mod = (c_act @ w_ada[l] + b_ada[l])[:, None, :]
        shift_m, scale_m, gate_m, shift_f, scale_f, gate_f = jnp.split(mod, 6, axis=-1)
        h = rms_norm(x, pre_norm_mix[l]) * (1.0 + scale_m) + shift_m
        y = hybrid_mixer(h, positions, log_gamma, w_in[l], conv_w[l], a_log[l], dt_bias[l],
                         dn_norm_w[l], rt_norm_w[l], w_out[l])
        x = x + gate_m * rms_norm(y, post_norm_mix[l])
        h = rms_norm(x, pre_norm_ffn[l]) * (1.0 + scale_f) + shift_f
        y = moe_ffn(h, w_router[l], router_bias[l], w_gate_exp[l], w_up_exp[l], w_down_exp[l],
                    w_gate_sh[l], w_up_sh[l], w_down_sh[l])
        x = x + gate_f * rms_norm(y, post_norm_ffn[l])
    return x
```

```python
import functools
import math

import numpy as np
import jax
import jax.numpy as jnp
from jax import lax
from jax.experimental import pallas as pl
from jax.experimental.pallas import tpu as pltpu

F32 = jnp.float32
BF16 = jnp.bfloat16

D_MODEL = 2048
DN_HEADS = 8
DN_DK = 128
DN_DV = 128
DN_CONV = 4
DN_CHUNK = 64
RT_HEADS = 8
RT_DK = 64
RT_DV = 128
ROPE_BASE = 10000.0
DN_WIDTH = DN_HEADS * DN_DV
RT_WIDTH = RT_HEADS * RT_DV
RT_QK = RT_HEADS * RT_DK
N_EXPERTS = 64
TOP_K = 8
N_GROUPS = 8
TOPK_GROUPS = 4
EXPERT_FF = 512
SHARED_FF = 512
ROUTED_SCALE = 2.5
NORM_EPS = 1e-6

LANES = 128
VMEM_LIMIT = 56 * 1024 * 1024

PROJ_WIDTH = 3 * DN_WIDTH + DN_WIDTH + 2 * RT_QK + 2 * RT_WIDTH
DN_ROWS = 256
RT_CHUNK = 256
MOE_TILE = 256


def _silu(x):
    return x * jax.nn.sigmoid(x)


def _softplus(x):
    return jnp.maximum(x, 0.0) + jnp.log1p(jnp.exp(-jnp.abs(x)))


def _dot(a, b, precision=None):
    return jnp.dot(a, b, preferred_element_type=F32, precision=precision)


def _dot_nt(a, b, precision=None):
    return lax.dot_general(a, b, (((1,), (1,)), ((), ())), preferred_element_type=F32, precision=precision)


def _dot_tn(a, b, precision=None):
    return lax.dot_general(a, b, (((0,), (0,)), ((), ())), preferred_element_type=F32, precision=precision)


def _rms(x, w):
    return x * lax.rsqrt(jnp.mean(x * x, axis=-1, keepdims=True) + NORM_EPS) * w


def _params(sem):
    return pltpu.CompilerParams(dimension_semantics=sem, vmem_limit_bytes=VMEM_LIMIT)


def _ada_kernel(c_ref, w_ref, b_ref, o_ref):
    ca = _silu(c_ref[...])
    o_ref[...] = _dot(ca.astype(BF16), w_ref[...].astype(BF16)) + b_ref[...]


def _ada(c_pad, w_ada, b_ada, tn=1024):
    m, k = c_pad.shape
    n = w_ada.shape[1]
    return pl.pallas_call(
        _ada_kernel,
        grid=(n // tn,),
        in_specs=[pl.BlockSpec((m, k), lambda j: (0, 0)),
                  pl.BlockSpec((k, tn), lambda j: (0, j)),
                  pl.BlockSpec((1, tn), lambda j: (0, j))],
        out_specs=pl.BlockSpec((m, tn), lambda j: (0, j)),
        out_shape=jax.ShapeDtypeStruct((m, n), F32),
        compiler_params=_params(("arbitrary",)),
        name="ada_mod",
    )(c_pad, w_ada, b_ada)


def _inproj_kernel(x_ref, mod_ref, nw_ref, w_ref, wab_ref, o_ref, ab_ref, h_ref):
    j = pl.program_id(1)

    @pl.when(j == 0)
    def _():
        x = x_ref[...]
        h = _rms(x, nw_ref[...]) * (1.0 + mod_ref[1:2, :]) + mod_ref[0:1, :]
        hb = h.astype(BF16)
        h_ref[...] = hb
        ab_ref[...] = _dot(hb, wab_ref[...])

    o_ref[...] = _dot(h_ref[...], w_ref[...]).astype(o_ref.dtype)


def _inproj(x2d, mod3, nw, w_main, w_ab, seq, tm=1024, tn=1024):
    t, d = x2d.shape
    n = w_main.shape[1]
    tm = min(tm, seq)
    per_b = seq // tm
    return pl.pallas_call(
        _inproj_kernel,
        grid=(t // tm, n // tn),
        in_specs=[pl.BlockSpec((tm, d), lambda i, j: (i, 0)),
                  pl.BlockSpec((None, 8, d), lambda i, j: (i // per_b, 0, 0)),
                  pl.BlockSpec((1, d), lambda i, j: (0, 0)),
                  pl.BlockSpec((d, tn), lambda i, j: (0, j)),
                  pl.BlockSpec((d, 2 * LANES), lambda i, j: (0, 0))],
        out_specs=[pl.BlockSpec((tm, tn), lambda i, j: (i, j)),
                   pl.BlockSpec((tm, 2 * LANES), lambda i, j: (i, 0))],
        out_shape=[jax.ShapeDtypeStruct((t, n), BF16),
                   jax.ShapeDtypeStruct((t, 2 * LANES), F32)],
        scratch_shapes=[pltpu.VMEM((tm, d), BF16)],
        compiler_params=_params(("arbitrary", "arbitrary")),
        name="in_proj",
    )(x2d, mod3, nw, w_main, w_ab)


def _inv_unit_lower(a, eye, c):
    x = eye - a
    p = _dot(a, a)
    k = 2
    while k < c:
        x = x + _dot(x, p)
        k *= 2
        if k < c:
            p = _dot(p, p)
    return x


def _dn_kernel(qkv_ref, z_ref, ab_ref, cw_ref, alog_ref, dtb_ref, nw_ref, o_ref,
               ext_ref, q_s, k_s, v_s, g_s, b_s, st_ref, *, rows, chunk):
    n = pl.program_id(1)
    w3 = 3 * DN_WIDTH

    @pl.when(n == 0)
    def _():
        ext_ref[0:8, :] = jnp.zeros((8, w3), F32)
        st_ref[...] = jnp.zeros_like(st_ref)

    x = qkv_ref[...].astype(F32)
    ext_ref[8:8 + rows, :] = x
    cw = cw_ref[...]
    acc = x * cw[3:4, :]
    for j in range(DN_CONV - 1):
        acc = acc + ext_ref[5 + j:5 + j + rows, :] * cw[j:j + 1, :]
    ext_ref[0:8, :] = x[rows - 8:rows, :]
    act = _silu(acc)

    scale = DN_DK ** -0.5
    for h in range(DN_HEADS):
        sl = slice(h * DN_DK, (h + 1) * DN_DK)
        qh = act[:, sl]
        kh = act[:, DN_WIDTH + h * DN_DK:DN_WIDTH + (h + 1) * DN_DK]
        q_s[:, sl] = qh * (lax.rsqrt(jnp.sum(qh * qh, axis=-1, keepdims=True) + NORM_EPS) * scale)
        k_s[:, sl] = kh * lax.rsqrt(jnp.sum(kh * kh, axis=-1, keepdims=True) + NORM_EPS)
    v_s[...] = act[:, 2 * DN_WIDTH:]

    ab = ab_ref[...]
    g_s[...] = -jnp.exp(alog_ref[...]) * _softplus(ab[:, :LANES] + dtb_ref[...])
    b_s[...] = jax.nn.sigmoid(ab[:, LANES:])

    c = chunk
    ri = lax.broadcasted_iota(jnp.int32, (c, c), 0)
    ci = lax.broadcasted_iota(jnp.int32, (c, c), 1)
    causal = ri >= ci
    strict = ri > ci
    ltri = causal.astype(F32)
    utri = (ri <= ci).astype(F32)
    eye = (ri == ci).astype(F32)
    nw = nw_ref[...]
    hp = lax.Precision.HIGHEST

    def chunk_body(ic, carry):
        r0 = pl.multiple_of(ic * c, c)
        g_c = g_s[pl.ds(r0, c), :]
        b_c = b_s[pl.ds(r0, c), :]
        gc = _dot(ltri, g_c, hp)
        gct = _dot_tn(g_c, utri, hp)
        glast = gc[c - 1:c, :]
        eg = jnp.exp(gc)
        ekd = jnp.exp(glast - gc)
        cdv = jnp.exp(glast)
        for h in range(DN_HEADS):
            sl = slice(h * DN_DK, (h + 1) * DN_DK)
            q = q_s[pl.ds(r0, c), sl]
            k = k_s[pl.ds(r0, c), sl]
            v = v_s[pl.ds(r0, c), sl]
            diff = gc[:, h:h + 1] - gct[h:h + 1, :]
            decay = jnp.where(causal, jnp.exp(jnp.where(causal, diff, 0.0)), 0.0)
            bcol = b_c[:, h:h + 1]
            kb = k * bcol
            a_mat = jnp.where(strict, _dot_nt(kb, k) * decay, 0.0)
            tinv = _inv_unit_lower(a_mat, eye, c)
            rhs = jnp.concatenate([kb * eg[:, h:h + 1], v * bcol], axis=1)
            sol = _dot(tinv, rhs)
            w_c = sol[:, :DN_DK]
            u_c = sol[:, DN_DK:]
            attn = jnp.where(causal, _dot_nt(q, k) * decay, 0.0)
            s = st_ref[h]
            v_new = u_c - _dot(w_c, s)
            out = _dot(q * eg[:, h:h + 1], s) + _dot(attn, v_new)
            st_ref[h] = s * cdv[:, h:h + 1] + _dot_tn(k * ekd[:, h:h + 1], v_new)
            zz = z_ref[pl.ds(r0, c), sl].astype(F32)
            o_ref[pl.ds(r0, c), sl] = (_rms(out, nw) * _silu(zz)).astype(o_ref.dtype)
        return carry

    lax.fori_loop(0, rows // c, chunk_body, 0)


def _deltanet(proj, ab, conv_w, alog, dtb, nw, batch, seq, rows=DN_ROWS, chunk=DN_CHUNK):
    rows = min(rows, seq)
    nb = seq // rows
    t = batch * seq
    w3 = 3 * DN_WIDTH
    kern = functools.partial(_dn_kernel, rows=rows, chunk=chunk)
    return pl.pallas_call(
        kern,
        grid=(batch, nb),
        in_specs=[pl.BlockSpec((rows, w3), lambda b, n: (b * nb + n, 0)),
                  pl.BlockSpec((rows, DN_WIDTH), lambda b, n: (b * nb + n, 3)),
                  pl.BlockSpec((rows, 2 * LANES), lambda b, n: (b * nb + n, 0)),
                  pl.BlockSpec((DN_CONV, w3), lambda b, n: (0, 0)),
                  pl.BlockSpec((1, LANES), lambda b, n: (0, 0)),
                  pl.BlockSpec((1, LANES), lambda b, n: (0, 0)),
                  pl.BlockSpec((1, DN_DV), lambda b, n: (0, 0))],
        out_specs=pl.BlockSpec((rows, DN_WIDTH), lambda b, n: (b * nb + n, 0)),
        out_shape=jax.ShapeDtypeStruct((t, DN_WIDTH), BF16),
        scratch_shapes=[pltpu.VMEM((rows + 8, w3), F32),
                        pltpu.VMEM((rows, DN_WIDTH), F32),
                        pltpu.VMEM((rows, DN_WIDTH), F32),
                        pltpu.VMEM((rows, DN_WIDTH), F32),
                        pltpu.VMEM((rows, LANES), F32),
                        pltpu.VMEM((rows, LANES), F32),
                        pltpu.VMEM((DN_HEADS, DN_DK, DN_DV), F32)],
        compiler_params=_params(("arbitrary", "arbitrary")),
        name="deltanet",
    )(proj, proj, ab, conv_w, alog, dtb, nw)


def _rt_log_gamma(idx):
    return jnp.log(1.0 - jnp.exp2(-5.0 - idx.astype(F32)))


def _rt_kernel(qk_ref, v_ref, g_ref, cos_ref, sin_ref, nw_ref, o_ref, st_ref, msk_ref, *, chunk):
    n = pl.program_id(1)
    c = chunk
    half = RT_QK // 2

    @pl.when(n == 0)
    def _():
        st_ref[...] = jnp.zeros_like(st_ref)
        rh = (lax.broadcasted_iota(jnp.int32, (RT_QK, RT_WIDTH), 0) & (half - 1)) >> 5
        ch = lax.broadcasted_iota(jnp.int32, (RT_QK, RT_WIDTH), 1) >> 7
        msk_ref[...] = jnp.where(rh == ch, jnp.exp(float(c) * _rt_log_gamma(rh)), 0.0)

    cos = cos_ref[...]
    sin = sin_ref[...]
    qk = qk_ref[...].astype(F32)
    q1, q2 = qk[:, 0:half], qk[:, half:2 * half]
    k1, k2 = qk[:, 2 * half:3 * half], qk[:, 3 * half:4 * half]
    kscale = RT_DK ** -0.5
    q = jnp.concatenate([q1 * cos - q2 * sin, q2 * cos + q1 * sin], axis=1)
    k = jnp.concatenate([k1 * cos - k2 * sin, k2 * cos + k1 * sin], axis=1) * kscale
    v = v_ref[...].astype(F32)

    lane_h = (lax.broadcasted_iota(jnp.int32, (1, RT_QK), 1) & (half - 1)) >> 5
    lg_lane = _rt_log_gamma(lane_h)
    row = lax.broadcasted_iota(jnp.int32, (c, 1), 0).astype(F32)
    q_dec = q * jnp.exp((row + 1.0) * lg_lane)
    k_dec = k * jnp.exp((float(c) - 1.0 - row) * lg_lane)

    ri = lax.broadcasted_iota(jnp.int32, (c, c), 0)
    ci = lax.broadcasted_iota(jnp.int32, (c, c), 1)
    causal = ri >= ci
    rel = jnp.where(causal, (ri - ci).astype(F32), 0.0)

    s_all = st_ref[...]
    inter = _dot(q_dec, s_all)
    msk = msk_ref[...]
    st_ref[...] = s_all * msk + jnp.where(msk > 0.0, _dot_tn(k_dec, v), 0.0)

    nw = nw_ref[...]
    for h in range(RT_HEADS):
        sl = slice(h * RT_DV, (h + 1) * RT_DV)
        lg = math.log(1.0 - 2.0 ** (-5.0 - h))
        d_h = jnp.where(causal, jnp.exp(rel * lg), 0.0)
        qm = jnp.where(lane_h == h, q, 0.0)
        sc = _dot_nt(qm, k) * d_h
        o = _dot(sc, v[:, sl]) + inter[:, sl]
        mu = jnp.mean(o, axis=-1, keepdims=True)
        var = jnp.mean(jnp.square(o - mu), axis=-1, keepdims=True)
        y = (o - mu) * lax.rsqrt(var + NORM_EPS) * nw[:, sl]
        gg = g_ref[:, sl].astype(F32)
        o_ref[:, sl] = (y * _silu(gg)).astype(o_ref.dtype)


def _retention(proj, cos, sin, nw, batch, seq, chunk=RT_CHUNK):
    chunk = min(chunk, seq)
    nb = seq // chunk
    t = batch * seq
    kern = functools.partial(_rt_kernel, chunk=chunk)
    return pl.pallas_call(
        kern,
        grid=(batch, nb),
        in_specs=[pl.BlockSpec((chunk, 2 * RT_QK), lambda b, n: (b * nb + n, 4)),
                  pl.BlockSpec((chunk, RT_WIDTH), lambda b, n: (b * nb + n, 5)),
                  pl.BlockSpec((chunk, RT_WIDTH), lambda b, n: (b * nb + n, 6)),
                  pl.BlockSpec((chunk, RT_QK // 2), lambda b, n: (b * nb + n, 0)),
                  pl.BlockSpec((chunk, RT_QK // 2), lambda b, n: (b * nb + n, 0)),
                  pl.BlockSpec((1, RT_WIDTH), lambda b, n: (0, 0))],
        out_specs=pl.BlockSpec((chunk, RT_WIDTH), lambda b, n: (b * nb + n, 0)),
        out_shape=jax.ShapeDtypeStruct((t, RT_WIDTH), BF16),
        scratch_shapes=[pltpu.VMEM((RT_QK, RT_WIDTH), F32),
                        pltpu.VMEM((RT_QK, RT_WIDTH), F32)],
        compiler_params=_params(("arbitrary", "arbitrary")),
        name="retention",
    )(proj, proj, proj, cos, sin, nw)


def _outproj_kernel(oa_ref, ob_ref, wa_ref, wb_ref, x_ref, mod_ref, pn_ref, fn_ref, wr_ref,
                    x1_ref, h2_ref, lg_ref):
    y = _dot(oa_ref[...], wa_ref[...]) + _dot(ob_ref[...], wb_ref[...])
    x1 = x_ref[...] + mod_ref[2:3, :] * _rms(y, pn_ref[...])
    x1_ref[...] = x1
    h2 = _rms(x1, fn_ref[...]) * (1.0 + mod_ref[4:5, :]) + mod_ref[3:4, :]
    h2_ref[...] = h2.astype(h2_ref.dtype)
    h_hi = h2.astype(BF16)
    h_lo = (h2 - h_hi.astype(F32)).astype(BF16)
    wr = wr_ref[...]
    w_hi = wr.astype(BF16)
    w_lo = (wr - w_hi.astype(F32)).astype(BF16)
    lg_ref[...] = _dot(h_hi, w_hi) + (_dot(h_hi, w_lo) + _dot(h_lo, w_hi))


def _outproj(o_a, o_b, wa, wb, x2d, mod3, pn, fn, wr, seq, tm=512):
    t, d = x2d.shape
    tm = min(tm, seq)
    per_b = seq // tm
    return pl.pallas_call(
        _outproj_kernel,
        grid=(t // tm,),
        in_specs=[pl.BlockSpec((tm, DN_WIDTH), lambda i: (i, 0)),
                  pl.BlockSpec((tm, RT_WIDTH), lambda i: (i, 0)),
                  pl.BlockSpec((DN_WIDTH, d), lambda i: (0, 0)),
                  pl.BlockSpec((RT_WIDTH, d), lambda i: (0, 0)),
                  pl.BlockSpec((tm, d), lambda i: (i, 0)),
                  pl.BlockSpec((None, 8, d), lambda i: (i // per_b, 0, 0)),
                  pl.BlockSpec((1, d), lambda i: (0, 0)),
                  pl.BlockSpec((1, d), lambda i: (0, 0)),
                  pl.BlockSpec((d, LANES), lambda i: (0, 0))],
        out_specs=[pl.BlockSpec((tm, d), lambda i: (i, 0)),
                   pl.BlockSpec((tm, d), lambda i: (i, 0)),
                   pl.BlockSpec((tm, LANES), lambda i: (i, 0))],
        out_shape=[jax.ShapeDtypeStruct((t, d), F32),
                   jax.ShapeDtypeStruct((t, d), BF16),
                   jax.ShapeDtypeStruct((t, LANES), F32)],
        compiler_params=_params(("arbitrary",)),
        name="out_proj",
    )(o_a, o_b, wa, wb, x2d, mod3, pn, fn, wr)


def _moe_kernel(te_ref, tv_ref, xs_ref, wg_ref, wu_ref, wd_ref, ys_ref, wgb, wub, wdb):
    i = pl.program_id(0)
    e = te_ref[i]
    prev = te_ref[jnp.maximum(i - 1, 0)]

    @pl.when((i == 0) | (e != prev))
    def _():
        wgb[...] = wg_ref[...].astype(BF16)
        wub[...] = wu_ref[...].astype(BF16)
        wdb[...] = wd_ref[...].astype(BF16)

    @pl.when(tv_ref[i] > 0)
    def _():
        x = xs_ref[...]
        a = _silu(_dot(x, wgb[...])) * _dot(x, wub[...])
        ys_ref[...] = _dot(a.astype(BF16), wdb[...]).astype(ys_ref.dtype)

    @pl.when(tv_ref[i] == 0)
    def _():
        ys_ref[...] = jnp.zeros_like(ys_ref)


def _moe_experts(tile_e, tile_v, xs, wg, wu, wd, tm=MOE_TILE):
    n_rows, d = xs.shape
    ff = wg.shape[-1]
    n_tiles = n_rows // tm
    grid_spec = pltpu.PrefetchScalarGridSpec(
        num_scalar_prefetch=2,
        grid=(n_tiles,),
        in_specs=[pl.BlockSpec((tm, d), lambda i, te, tv: (i, 0)),
                  pl.BlockSpec((None, d, ff), lambda i, te, tv: (te[i], 0, 0)),
                  pl.BlockSpec((None, d, ff), lambda i, te, tv: (te[i], 0, 0)),
                  pl.BlockSpec((None, ff, d), lambda i, te, tv: (te[i], 0, 0))],
        out_specs=pl.BlockSpec((tm, d), lambda i, te, tv: (i, 0)),
        scratch_shapes=[pltpu.VMEM((d, ff), BF16), pltpu.VMEM((d, ff), BF16), pltpu.VMEM((ff, d), BF16)],
    )
    return pl.pallas_call(
        _moe_kernel,
        grid_spec=grid_spec,
        out_shape=jax.ShapeDtypeStruct((n_rows, d), BF16),
        compiler_params=_params(("arbitrary",)),
        name="moe_experts",
    )(tile_e, tile_v, xs, wg, wu, wd)


def _final_kernel(h2_ref, rt_ref, x1_ref, mod_ref, pn_ref, wg_ref, wu_ref, wd_ref, o_ref):
    h = h2_ref[...]
    a = _silu(_dot(h, wg_ref[...])) * _dot(h, wu_ref[...])
    y = _dot(a.astype(BF16), wd_ref[...]) + rt_ref[...].astype(F32)
    o_ref[...] = x1_ref[...] + mod_ref[5:6, :] * _rms(y, pn_ref[...])


def _final(h2, routed, x1, mod3, pn, wg, wu, wd, seq, tm=512):
    t, d = x1.shape
    ff = wg.shape[-1]
    tm = min(tm, seq)
    per_b = seq // tm
    return pl.pallas_call(
        _final_kernel,
        grid=(t // tm,),
        in_specs=[pl.BlockSpec((tm, d), lambda i: (i, 0)),
                  pl.BlockSpec((tm, d), lambda i: (i, 0)),
                  pl.BlockSpec((tm, d), lambda i: (i, 0)),
                  pl.BlockSpec((None, 8, d), lambda i: (i // per_b, 0, 0)),
                  pl.BlockSpec((1, d), lambda i: (0, 0)),
                  pl.BlockSpec((d, ff), lambda i: (0, 0)),
                  pl.BlockSpec((d, ff), lambda i: (0, 0)),
                  pl.BlockSpec((ff, d), lambda i: (0, 0))],
        out_specs=pl.BlockSpec((tm, d), lambda i: (i, 0)),
        out_shape=jax.ShapeDtypeStruct((t, d), F32),
        compiler_params=_params(("arbitrary",)),
        name="shared_final",
    )(h2, routed, x1, mod3, pn, wg, wu, wd)


def _route(logits, router_bias, n_tok, tm=MOE_TILE):
    scores = jax.nn.sigmoid(logits)
    sel = scores + router_bias.astype(F32)
    grp = sel.reshape(n_tok, N_GROUPS, N_EXPERTS // N_GROUPS)
    grp_score = jnp.sum(lax.top_k(grp, 2)[0], axis=-1)
    _, g_idx = lax.top_k(grp_score, TOPK_GROUPS)
    g_keep = jnp.any(g_idx[:, :, None] == jnp.arange(N_GROUPS)[None, None, :], axis=1)
    sel = jnp.where(jnp.repeat(g_keep, N_EXPERTS // N_GROUPS, axis=1), sel, -jnp.inf)
    _, e_idx = lax.top_k(sel, TOP_K)
    wts = jnp.take_along_axis(scores, e_idx, axis=1)
    wts = wts / jnp.sum(wts, axis=-1, keepdims=True) * ROUTED_SCALE

    n_assign = n_tok * TOP_K
    flat_e = e_idx.reshape(n_assign)
    order = jnp.argsort(flat_e)
    se = flat_e[order]
    stok = (order // TOP_K).astype(jnp.int32)
    counts = jnp.zeros((N_EXPERTS,), jnp.int32).at[flat_e].add(1)
    starts = jnp.cumsum(counts) - counts
    padded = (counts + tm - 1) // tm * tm
    pad_end = jnp.cumsum(padded)
    pad_start = pad_end - padded
    dest_sorted = pad_start[se] + (jnp.arange(n_assign, dtype=jnp.int32) - starts[se])
    n_tiles = -(-n_assign // tm) + N_EXPERTS
    n_rows = n_tiles * tm
    row_tok = jnp.zeros((n_rows,), jnp.int32).at[dest_sorted].set(stok)
    dest = jnp.zeros((n_assign,), jnp.int32).at[order].set(dest_sorted).reshape(n_tok, TOP_K)
    tile_start = jnp.arange(n_tiles, dtype=jnp.int32) * tm
    tile_e = jnp.minimum(jnp.searchsorted(pad_end, tile_start, side='right'), N_EXPERTS - 1).astype(jnp.int32)
    tile_v = (tile_start < pad_end[-1]).astype(jnp.int32)
    return wts, row_tok, dest, tile_e, tile_v


def kernel(x, c, positions, w_ada, b_ada, pre_norm_mix, post_norm_mix, w_in, conv_w, a_log, dt_bias,
           dn_norm_w, rt_norm_w, w_out, pre_norm_ffn, post_norm_ffn, w_router, router_bias,
           w_gate_exp, w_up_exp, w_down_exp, w_gate_sh, w_up_sh, w_down_sh):
    batch, seq, d = x.shape
    t = batch * seq
    depth = w_ada.shape[0]
    x2d = x.reshape(t, d)

    half = RT_DK // 2
    theta = 1.0 / (ROPE_BASE ** jnp.linspace(0.0, 1.0, half, dtype=F32))
    ang = positions.astype(F32).reshape(t, 1) * theta[None, :]
    cos = jnp.tile(jnp.cos(ang), (1, RT_HEADS))
    sin = jnp.tile(jnp.sin(ang), (1, RT_HEADS))

    c_pad = jnp.zeros((8, d), F32).at[:batch].set(c)
    hd = np.arange(RT_HEADS)[:, None] * RT_DK + np.arange(half)[None, :]
    rot_perm = np.concatenate([hd.reshape(-1), (hd + half).reshape(-1)])

    for l in range(depth):
        mod = _ada(c_pad, w_ada[l], b_ada[l].reshape(1, -1))
        mod3 = jnp.zeros((batch, 8, d), F32).at[:, :6].set(mod[:batch].reshape(batch, 6, d))

        wl = w_in[l]
        o = 0
        segs = []
        for wdt in (DN_WIDTH, DN_WIDTH, DN_WIDTH, DN_WIDTH, DN_HEADS, DN_HEADS, RT_QK, RT_QK, RT_WIDTH, RT_WIDTH):
            segs.append(wl[:, o:o + wdt])
            o += wdt
        dq, dk, dv, dz, da, db, rq, rk, rv, rg = segs
        w_main = jnp.concatenate([dq, dk, dv, dz, rq[:, rot_perm], rk[:, rot_perm], rv, rg], axis=1).astype(BF16)
        w_ab = (jnp.zeros((d, 2 * LANES), F32).at[:, :DN_HEADS].set(da)
                .at[:, LANES:LANES + DN_HEADS].set(db)).astype(BF16)

        proj, ab = _inproj(x2d, mod3, pre_norm_mix[l].reshape(1, d), w_main, w_ab, seq)

        alog = jnp.zeros((1, LANES), F32).at[0, :DN_HEADS].set(a_log[l])
        dtb = jnp.zeros((1, LANES), F32).at[0, :DN_HEADS].set(dt_bias[l])
        o_a = _deltanet(proj, ab, conv_w[l], alog, dtb, dn_norm_w[l].reshape(1, DN_DV), batch, seq)
        o_b = _retention(proj, cos, sin, rt_norm_w[l].reshape(1, RT_WIDTH), batch, seq)

        wo = w_out[l].astype(BF16)
        wr = jnp.zeros((d, LANES), F32).at[:, :N_EXPERTS].set(w_router[l])
        x1, h2, logits = _outproj(o_a, o_b, wo[:DN_WIDTH], wo[DN_WIDTH:], x2d, mod3,
                                  post_norm_mix[l].reshape(1, d), pre_norm_ffn[l].reshape(1, d), wr, seq)

        wts, row_tok, dest, tile_e, tile_v = _route(logits[:, :N_EXPERTS], router_bias[l], t)
        xs = jnp.take(h2, row_tok, axis=0)
        ys = _moe_experts(tile_e, tile_v, xs, w_gate_exp[l], w_up_exp[l], w_down_exp[l])
        routed = jnp.sum(jnp.take(ys, dest, axis=0).astype(F32) * wts[:, :, None], axis=1)

        x2d = _final(h2, routed, x1, mod3, post_norm_ffn[l].reshape(1, d),
                     w_gate_sh[l].astype(BF16), w_up_sh[l].astype(BF16), w_down_sh[l].astype(BF16), seq)
    return x2d.reshape(batch, seq, d)
```

```python
import functools
import math

import numpy as np
import jax
import jax.numpy as jnp
from jax import lax
from jax.experimental import pallas as pl
from jax.experimental.pallas import tpu as pltpu

F32 = jnp.float32
BF16 = jnp.bfloat16

D_MODEL = 2048
DN_HEADS = 8
DN_DK = 128
DN_DV = 128
DN_CONV = 4
DN_CHUNK = 64
RT_HEADS = 8
RT_DK = 64
RT_DV = 128
ROPE_BASE = 10000.0
DN_WIDTH = DN_HEADS * DN_DV
RT_WIDTH = RT_HEADS * RT_DV
RT_QK = RT_HEADS * RT_DK
N_EXPERTS = 64
TOP_K = 8
N_GROUPS = 8
TOPK_GROUPS = 4
EXPERT_FF = 512
SHARED_FF = 512
ROUTED_SCALE = 2.5
NORM_EPS = 1e-6

LANES = 128
VMEM_LIMIT = 56 * 1024 * 1024

PROJ_WIDTH = 3 * DN_WIDTH + DN_WIDTH + 2 * RT_QK + 2 * RT_WIDTH
DN_ROWS = 256
RT_CHUNK = 256
MOE_TILE = 256


def _silu(x):
    return x * jax.nn.sigmoid(x)


def _softplus(x):
    return jnp.maximum(x, 0.0) + jnp.log1p(jnp.exp(-jnp.abs(x)))


def _dot(a, b, precision=None):
    return jnp.dot(a, b, preferred_element_type=F32, precision=precision)


def _dot_nt(a, b, precision=None):
    return lax.dot_general(a, b, (((1,), (1,)), ((), ())), preferred_element_type=F32, precision=precision)


def _dot_tn(a, b, precision=None):
    return lax.dot_general(a, b, (((0,), (0,)), ((), ())), preferred_element_type=F32, precision=precision)


def _rms(x, w):
    return x * lax.rsqrt(jnp.mean(x * x, axis=-1, keepdims=True) + NORM_EPS) * w


def _params(sem):
    return pltpu.CompilerParams(dimension_semantics=sem, vmem_limit_bytes=VMEM_LIMIT)


def _ada_kernel(c_ref, w_ref, b_ref, o_ref):
    ca = _silu(c_ref[...])
    o_ref[...] = _dot(ca.astype(BF16), w_ref[...].astype(BF16)) + b_ref[...]


def _ada(c_pad, w_ada, b_ada, tn=1024):
    m, k = c_pad.shape
    n = w_ada.shape[1]
    return pl.pallas_call(
        _ada_kernel,
        grid=(n // tn,),
        in_specs=[pl.BlockSpec((m, k), lambda j: (0, 0)),
                  pl.BlockSpec((k, tn), lambda j: (0, j)),
                  pl.BlockSpec((1, tn), lambda j: (0, j))],
        out_specs=pl.BlockSpec((m, tn), lambda j: (0, j)),
        out_shape=jax.ShapeDtypeStruct((m, n), F32),
        compiler_params=_params(("arbitrary",)),
        name="ada_mod",
    )(c_pad, w_ada, b_ada)


def _inproj_kernel(x_ref, mod_ref, nw_ref, w_ref, wab_ref, o_ref, ab_ref, h_ref):
    j = pl.program_id(1)

    @pl.when(j == 0)
    def _():
        x = x_ref[...]
        h = _rms(x, nw_ref[...]) * (1.0 + mod_ref[1:2, :]) + mod_ref[0:1, :]
        hb = h.astype(BF16)
        h_ref[...] = hb
        ab_ref[...] = _dot(hb, wab_ref[...])

    o_ref[...] = _dot(h_ref[...], w_ref[...]).astype(o_ref.dtype)


def _inproj(x2d, mod3, nw, w_main, w_ab, seq, tm=1024, tn=1024):
    t, d = x2d.shape
    n = w_main.shape[1]
    tm = min(tm, seq)
    per_b = seq // tm
    return pl.pallas_call(
        _inproj_kernel,
        grid=(t // tm, n // tn),
        in_specs=[pl.BlockSpec((tm, d), lambda i, j: (i, 0)),
                  pl.BlockSpec((None, 8, d), lambda i, j: (i // per_b, 0, 0)),
                  pl.BlockSpec((1, d), lambda i, j: (0, 0)),
                  pl.BlockSpec((d, tn), lambda i, j: (0, j)),
                  pl.BlockSpec((d, 2 * LANES), lambda i, j: (0, 0))],
        out_specs=[pl.BlockSpec((tm, tn), lambda i, j: (i, j)),
                   pl.BlockSpec((tm, 2 * LANES), lambda i, j: (i, 0))],
        out_shape=[jax.ShapeDtypeStruct((t, n), BF16),
                   jax.ShapeDtypeStruct((t, 2 * LANES), F32)],
        scratch_shapes=[pltpu.VMEM((tm, d), BF16)],
        compiler_params=_params(("arbitrary", "arbitrary")),
        name="in_proj",
    )(x2d, mod3, nw, w_main, w_ab)


def _inv_unit_lower(a, eye, c):
    x = eye - a
    p = _dot(a, a)
    k = 2
    while k < c:
        x = x + _dot(x, p)
        k *= 2
        if k < c:
            p = _dot(p, p)
    return x


def _dn_kernel(qkv_ref, z_ref, ab_ref, cw_ref, alog_ref, dtb_ref, nw_ref, o_ref,
               ext_ref, q_s, k_s, v_s, g_s, b_s, st_ref, *, rows, chunk):
    n = pl.program_id(1)
    w3 = 3 * DN_WIDTH

    @pl.when(n == 0)
    def _():
        ext_ref[0:8, :] = jnp.zeros((8, w3), F32)
        st_ref[...] = jnp.zeros_like(st_ref)

    x = qkv_ref[...].astype(F32)
    ext_ref[8:8 + rows, :] = x
    cw = cw_ref[...]
    acc = x * cw[3:4, :]
    for j in range(DN_CONV - 1):
        acc = acc + ext_ref[5 + j:5 + j + rows, :] * cw[j:j + 1, :]
    ext_ref[0:8, :] = x[rows - 8:rows, :]
    act = _silu(acc)

    scale = DN_DK ** -0.5
    for h in range(DN_HEADS):
        sl = slice(h * DN_DK, (h + 1) * DN_DK)
        qh = act[:, sl]
        kh = act[:, DN_WIDTH + h * DN_DK:DN_WIDTH + (h + 1) * DN_DK]
        q_s[:, sl] = qh * (lax.rsqrt(jnp.sum(qh * qh, axis=-1, keepdims=True) + NORM_EPS) * scale)
        k_s[:, sl] = kh * lax.rsqrt(jnp.sum(kh * kh, axis=-1, keepdims=True) + NORM_EPS)
    v_s[...] = act[:, 2 * DN_WIDTH:]

    ab = ab_ref[...]
    g_s[...] = -jnp.exp(alog_ref[...]) * _softplus(ab[:, :LANES] + dtb_ref[...])
    b_s[...] = jax.nn.sigmoid(ab[:, LANES:])

    c = chunk
    ri = lax.broadcasted_iota(jnp.int32, (c, c), 0)
    ci = lax.broadcasted_iota(jnp.int32, (c, c), 1)
    causal = ri >= ci
    strict = ri > ci
    ltri = causal.astype(F32)
    utri = (ri <= ci).astype(F32)
    eye = (ri == ci).astype(F32)
    nw = nw_ref[...]
    hp = lax.Precision.HIGHEST

    def chunk_body(ic, carry):
        r0 = pl.multiple_of(ic * c, c)
        g_c = g_s[pl.ds(r0, c), :]
        b_c = b_s[pl.ds(r0, c), :]
        gc = _dot(ltri, g_c, hp)
        gct = _dot_tn(g_c, utri, hp)
        glast = gc[c - 1:c, :]
        eg = jnp.exp(gc)
        ekd = jnp.exp(glast - gc)
        cdv = jnp.exp(glast)
        for h in range(DN_HEADS):
            sl = slice(h * DN_DK, (h + 1) * DN_DK)
            q = q_s[pl.ds(r0, c), sl]
            k = k_s[pl.ds(r0, c), sl]
            v = v_s[pl.ds(r0, c), sl]
            diff = gc[:, h:h + 1] - gct[h:h + 1, :]
            decay = jnp.where(causal, jnp.exp(jnp.where(causal, diff, 0.0)), 0.0)
            bcol = b_c[:, h:h + 1]
            kb = k * bcol
            a_mat = jnp.where(strict, _dot_nt(kb, k) * decay, 0.0)
            tinv = _inv_unit_lower(a_mat, eye, c)
            rhs = jnp.concatenate([kb * eg[:, h:h + 1], v * bcol], axis=1)
            sol = _dot(tinv, rhs)
            w_c = sol[:, :DN_DK]
            u_c = sol[:, DN_DK:]
            attn = jnp.where(causal, _dot_nt(q, k) * decay, 0.0)
            s = st_ref[h]
            v_new = u_c - _dot(w_c, s)
            out = _dot(q * eg[:, h:h + 1], s) + _dot(attn, v_new)
            st_ref[h] = s * cdv[:, h:h + 1] + _dot_tn(k * ekd[:, h:h + 1], v_new)
            zz = z_ref[pl.ds(r0, c), sl].astype(F32)
            o_ref[pl.ds(r0, c), sl] = (_rms(out, nw) * _silu(zz)).astype(o_ref.dtype)
        return carry

    lax.fori_loop(0, rows // c, chunk_body, 0)


def _deltanet(proj, ab, conv_w, alog, dtb, nw, batch, seq, rows=DN_ROWS, chunk=DN_CHUNK):
    rows = min(rows, seq)
    nb = seq // rows
    t = batch * seq
    w3 = 3 * DN_WIDTH
    kern = functools.partial(_dn_kernel, rows=rows, chunk=chunk)
    return pl.pallas_call(
        kern,
        grid=(batch, nb),
        in_specs=[pl.BlockSpec((rows, w3), lambda b, n: (b * nb + n, 0)),
                  pl.BlockSpec((rows, DN_WIDTH), lambda b, n: (b * nb + n, 3)),
                  pl.BlockSpec((rows, 2 * LANES), lambda b, n: (b * nb + n, 0)),
                  pl.BlockSpec((DN_CONV, w3), lambda b, n: (0, 0)),
                  pl.BlockSpec((1, LANES), lambda b, n: (0, 0)),
                  pl.BlockSpec((1, LANES), lambda b, n: (0, 0)),
                  pl.BlockSpec((1, DN_DV), lambda b, n: (0, 0))],
        out_specs=pl.BlockSpec((rows, DN_WIDTH), lambda b, n: (b * nb + n, 0)),
        out_shape=jax.ShapeDtypeStruct((t, DN_WIDTH), BF16),
        scratch_shapes=[pltpu.VMEM((rows + 8, w3), F32),
                        pltpu.VMEM((rows, DN_WIDTH), F32),
                        pltpu.VMEM((rows, DN_WIDTH), F32),
                        pltpu.VMEM((rows, DN_WIDTH), F32),
                        pltpu.VMEM((rows, LANES), F32),
                        pltpu.VMEM((rows, LANES), F32),
                        pltpu.VMEM((DN_HEADS, DN_DK, DN_DV), F32)],
        compiler_params=_params(("arbitrary", "arbitrary")),
        name="deltanet",
    )(proj, proj, ab, conv_w, alog, dtb, nw)


def _rt_log_gamma(idx):
    return jnp.log(1.0 - jnp.exp2(-5.0 - idx.astype(F32)))


def _rt_kernel(qk_ref, v_ref, g_ref, cos_ref, sin_ref, nw_ref, o_ref, st_ref, msk_ref, *, chunk):
    n = pl.program_id(1)
    c = chunk
    half = RT_QK // 2

    @pl.when(n == 0)
    def _():
        st_ref[...] = jnp.zeros_like(st_ref)
        rh = (lax.broadcasted_iota(jnp.int32, (RT_QK, RT_WIDTH), 0) & (half - 1)) >> 5
        ch = lax.broadcasted_iota(jnp.int32, (RT_QK, RT_WIDTH), 1) >> 7
        msk_ref[...] = jnp.where(rh == ch, jnp.exp(float(c) * _rt_log_gamma(rh)), 0.0)

    cos = cos_ref[...]
    sin = sin_ref[...]
    qk = qk_ref[...].astype(F32)
    q1, q2 = qk[:, 0:half], qk[:, half:2 * half]
    k1, k2 = qk[:, 2 * half:3 * half], qk[:, 3 * half:4 * half]
    kscale = RT_DK ** -0.5
    q = jnp.concatenate([q1 * cos - q2 * sin, q2 * cos + q1 * sin], axis=1)
    k = jnp.concatenate([k1 * cos - k2 * sin, k2 * cos + k1 * sin], axis=1) * kscale
    v = v_ref[...].astype(F32)

    lane_h = (lax.broadcasted_iota(jnp.int32, (1, RT_QK), 1) & (half - 1)) >> 5
    lg_lane = _rt_log_gamma(lane_h)
    row = lax.broadcasted_iota(jnp.int32, (c, 1), 0).astype(F32)
    q_dec = q * jnp.exp((row + 1.0) * lg_lane)
    k_dec = k * jnp.exp((float(c) - 1.0 - row) * lg_lane)

    ri = lax.broadcasted_iota(jnp.int32, (c, c), 0)
    ci = lax.broadcasted_iota(jnp.int32, (c, c), 1)
    causal = ri >= ci
    rel = jnp.where(causal, (ri - ci).astype(F32), 0.0)

    s_all = st_ref[...]
    inter = _dot(q_dec, s_all)
    msk = msk_ref[...]
    st_ref[...] = s_all * msk + jnp.where(msk > 0.0, _dot_tn(k_dec, v), 0.0)

    nw = nw_ref[...]
    for h in range(RT_HEADS):
        sl = slice(h * RT_DV, (h + 1) * RT_DV)
        lg = math.log(1.0 - 2.0 ** (-5.0 - h))
        d_h = jnp.where(causal, jnp.exp(rel * lg), 0.0)
        qm = jnp.where(lane_h == h, q, 0.0)
        sc = _dot_nt(qm, k) * d_h
        o = _dot(sc, v[:, sl]) + inter[:, sl]
        mu = jnp.mean(o, axis=-1, keepdims=True)
        var = jnp.mean(jnp.square(o - mu), axis=-1, keepdims=True)
        y = (o - mu) * lax.rsqrt(var + NORM_EPS) * nw[:, sl]
        gg = g_ref[:, sl].astype(F32)
        o_ref[:, sl] = (y * _silu(gg)).astype(o_ref.dtype)


def _retention(proj, cos, sin, nw, batch, seq, chunk=RT_CHUNK):
    chunk = min(chunk, seq)
    nb = seq // chunk
    t = batch * seq
    kern = functools.partial(_rt_kernel, chunk=chunk)
    return pl.pallas_call(
        kern,
        grid=(batch, nb),
        in_specs=[pl.BlockSpec((chunk, 2 * RT_QK), lambda b, n: (b * nb + n, 4)),
                  pl.BlockSpec((chunk, RT_WIDTH), lambda b, n: (b * nb + n, 5)),
                  pl.BlockSpec((chunk, RT_WIDTH), lambda b, n: (b * nb + n, 6)),
                  pl.BlockSpec((chunk, RT_QK // 2), lambda b, n: (b * nb + n, 0)),
                  pl.BlockSpec((chunk, RT_QK // 2), lambda b, n: (b * nb + n, 0)),
                  pl.BlockSpec((1, RT_WIDTH), lambda b, n: (0, 0))],
        out_specs=pl.BlockSpec((chunk, RT_WIDTH), lambda b, n: (b * nb + n, 0)),
        out_shape=jax.ShapeDtypeStruct((t, RT_WIDTH), BF16),
        scratch_shapes=[pltpu.VMEM((RT_QK, RT_WIDTH), F32),
                        pltpu.VMEM((RT_QK, RT_WIDTH), F32)],
        compiler_params=_params(("arbitrary", "arbitrary")),
        name="retention",
    )(proj, proj, proj, cos, sin, nw)


def _pack_pair(a, b):
    au = lax.bitcast_convert_type(a.astype(BF16).astype(F32), jnp.uint32)
    bu = lax.bitcast_convert_type(b.astype(BF16).astype(F32), jnp.uint32)
    return (au >> 16) | bu


def _unpack_pair(w):
    a = lax.bitcast_convert_type(w << 16, F32)
    b = lax.bitcast_convert_type(w & jnp.uint32(0xFFFF0000), F32)
    return a, b


def _outproj_kernel(oa_ref, ob_ref, wa_ref, wb_ref, x_ref, mod_ref, pn_ref, fn_ref, wr_ref,
                    x1_ref, h2_ref, lg_ref):
    y = _dot(oa_ref[...], wa_ref[...]) + _dot(ob_ref[...], wb_ref[...])
    x1 = x_ref[...] + mod_ref[2:3, :] * _rms(y, pn_ref[...])
    x1_ref[...] = x1
    h2 = _rms(x1, fn_ref[...]) * (1.0 + mod_ref[4:5, :]) + mod_ref[3:4, :]
    hw = h2.shape[1] // 2
    h2_ref[...] = _pack_pair(h2[:, :hw], h2[:, hw:])
    h_hi = h2.astype(BF16)
    h_lo = (h2 - h_hi.astype(F32)).astype(BF16)
    wr = wr_ref[...]
    w_hi = wr.astype(BF16)
    w_lo = (wr - w_hi.astype(F32)).astype(BF16)
    lg_ref[...] = _dot_nt(w_hi, h_hi) + (_dot_nt(w_lo, h_hi) + _dot_nt(w_hi, h_lo))


def _outproj(o_a, o_b, wa, wb, x2d, mod3, pn, fn, wr_t, seq, tm=512):
    t, d = x2d.shape
    tm = min(tm, seq)
    per_b = seq // tm
    return pl.pallas_call(
        _outproj_kernel,
        grid=(t // tm,),
        in_specs=[pl.BlockSpec((tm, DN_WIDTH), lambda i: (i, 0)),
                  pl.BlockSpec((tm, RT_WIDTH), lambda i: (i, 0)),
                  pl.BlockSpec((DN_WIDTH, d), lambda i: (0, 0)),
                  pl.BlockSpec((RT_WIDTH, d), lambda i: (0, 0)),
                  pl.BlockSpec((tm, d), lambda i: (i, 0)),
                  pl.BlockSpec((None, 8, d), lambda i: (i // per_b, 0, 0)),
                  pl.BlockSpec((1, d), lambda i: (0, 0)),
                  pl.BlockSpec((1, d), lambda i: (0, 0)),
                  pl.BlockSpec((N_EXPERTS, d), lambda i: (0, 0))],
        out_specs=[pl.BlockSpec((tm, d), lambda i: (i, 0)),
                   pl.BlockSpec((tm, d // 2), lambda i: (i, 0)),
                   pl.BlockSpec((N_EXPERTS, tm), lambda i: (0, i))],
        out_shape=[jax.ShapeDtypeStruct((t, d), F32),
                   jax.ShapeDtypeStruct((t, d // 2), jnp.uint32),
                   jax.ShapeDtypeStruct((N_EXPERTS, t), F32)],
        compiler_params=_params(("arbitrary",)),
        name="out_proj",
    )(o_a, o_b, wa, wb, x2d, mod3, pn, fn, wr_t)


def _allreduce8(x, op):
    for sh in (1, 2, 4):
        x = op(x, pltpu.roll(x, sh, 0))
    return x


def _route_kernel(lg_ref, bias_ref, dest_ref, wts_ref, tmap_ref, cnt_s, base_s, ust_s, *, tn, tile):
    p = pl.program_id(0)
    j = pl.program_id(1)
    ng = N_GROUPS
    gs = N_EXPERTS // N_GROUPS
    n_tmap = tmap_ref.shape[1]
    hp = lax.Precision.HIGHEST

    @pl.when((p == 0) & (j == 0))
    def _():
        cnt_s[...] = jnp.zeros_like(cnt_s)
        base_s[...] = jnp.zeros_like(base_s)
        tmap_ref[...] = jnp.zeros_like(tmap_ref)
        r = lax.broadcasted_iota(jnp.int32, (tn, tn), 0)
        c = lax.broadcasted_iota(jnp.int32, (tn, tn), 1)
        ust_s[...] = jnp.where(r < c, 1.0, 0.0).astype(BF16)

    @pl.when((p == 1) & (j == 0))
    def _():
        cnt = cnt_s[...]
        padded = jnp.floor((cnt + float(tile - 1)) * (1.0 / tile)) * float(tile)
        ri = lax.broadcasted_iota(jnp.int32, (N_EXPERTS, N_EXPERTS), 0)
        ci = lax.broadcasted_iota(jnp.int32, (N_EXPERTS, N_EXPERTS), 1)
        pad_end = _dot(jnp.where(ri >= ci, 1.0, 0.0), padded, hp)
        base_s[...] = pad_end - padded
        cnt_s[...] = jnp.zeros_like(cnt_s)
        ts = lax.broadcasted_iota(jnp.int32, (N_EXPERTS, n_tmap), 1).astype(F32) * float(tile)
        ended = jnp.where(pad_end[:, 0:1] <= ts, 1.0, 0.0)
        te = jnp.minimum(jnp.sum(ended, axis=0, keepdims=True), float(N_EXPERTS - 1))
        total = pad_end[N_EXPERTS - 1:N_EXPERTS, 0:1]
        tv = jnp.where(ts[0:1, :] < total, 1, 0)
        row = lax.broadcasted_iota(jnp.int32, (8, n_tmap), 0)
        tmap_ref[...] = jnp.where(row == 0, te.astype(jnp.int32), jnp.where(row == 1, tv, 0))

    s_all = jax.nn.sigmoid(lg_ref[...])
    sel_all = s_all + bias_ref[...]
    sub = lax.broadcasted_iota(jnp.int32, (gs, tn), 0)
    neg = -jnp.inf

    vs, ss, gsc = [], [], []
    for g in range(ng):
        v = sel_all[g * gs:(g + 1) * gs, :]
        m1 = _allreduce8(v, jnp.maximum)
        first = _allreduce8(jnp.where(v == m1, sub, gs), jnp.minimum)
        m2 = _allreduce8(jnp.where(sub == first, neg, v), jnp.maximum)
        vs.append(v)
        ss.append(s_all[g * gs:(g + 1) * gs, :])
        gsc.append(m1 + m2)
    vm = []
    for g in range(ng):
        ahead = jnp.zeros((gs, tn), jnp.int32)
        for g2 in range(ng):
            if g2 == g:
                continue
            b = (gsc[g2] >= gsc[g]) if g2 < g else (gsc[g2] > gsc[g])
            ahead = ahead + jnp.where(b, 1, 0)
        vm.append(jnp.where(ahead < TOPK_GROUPS, vs[g], neg))
    firsts = []
    mem = [jnp.zeros((gs, tn), F32) for _ in range(ng)]
    w_rows = jnp.zeros((gs, tn), F32)
    for k in range(TOP_K):
        mx = vm[0]
        for g in range(1, ng):
            mx = jnp.maximum(mx, vm[g])
        mx = _allreduce8(mx, jnp.maximum)
        cand = jnp.full((gs, tn), N_EXPERTS, jnp.int32)
        for g in range(ng):
            cand = jnp.minimum(cand, jnp.where(vm[g] == mx, sub + g * gs, N_EXPERTS))
        first = _allreduce8(cand, jnp.minimum)
        wk = jnp.zeros((gs, tn), F32)
        for g in range(ng):
            hit = (sub + g * gs) == first
            vm[g] = jnp.where(hit, neg, vm[g])
            wk = wk + jnp.where(hit, ss[g], 0.0)
            mem[g] = mem[g] + jnp.where(hit, 1.0, 0.0)
        wk = _allreduce8(wk, jnp.add)
        w_rows = jnp.where(sub == k, wk, w_rows)
        firsts.append(first)
    wsum = _allreduce8(w_rows, jnp.add)
    wts_ref[...] = w_rows / wsum * ROUTED_SCALE

    m_all = jnp.concatenate(mem, axis=0)
    pref = _dot(m_all.astype(BF16), ust_s[...])
    tot = pref + (base_s[...] + cnt_s[...])[:, 0:1]
    d_rows = jnp.zeros((gs, tn), F32)
    for k in range(TOP_K):
        dk = jnp.zeros((gs, tn), F32)
        for g in range(ng):
            dk = dk + jnp.where((sub + g * gs) == firsts[k], tot[g * gs:(g + 1) * gs, :], 0.0)
        d_rows = jnp.where(sub == k, _allreduce8(dk, jnp.add), d_rows)
    dest_ref[...] = d_rows.astype(jnp.int32)
    cnt_s[...] = cnt_s[...] + jnp.sum(m_all, axis=1, keepdims=True)


def _route(logits_t, bias_col, n_tmap, tn=512, tile=MOE_TILE):
    ne, t = logits_t.shape
    tn = min(tn, t)
    nb = t // tn
    kern = functools.partial(_route_kernel, tn=tn, tile=tile)
    return pl.pallas_call(
        kern,
        grid=(2, nb),
        in_specs=[pl.BlockSpec((ne, tn), lambda p, j: (0, j)),
                  pl.BlockSpec((ne, 1), lambda p, j: (0, 0))],
        out_specs=[pl.BlockSpec((TOP_K, tn), lambda p, j: (0, j * p)),
                   pl.BlockSpec((TOP_K, tn), lambda p, j: (0, j * p)),
                   pl.BlockSpec((8, n_tmap), lambda p, j: (0, 0))],
        out_shape=[jax.ShapeDtypeStruct((TOP_K, t), jnp.int32),
                   jax.ShapeDtypeStruct((TOP_K, t), F32),
                   jax.ShapeDtypeStruct((8, n_tmap), jnp.int32)],
        scratch_shapes=[pltpu.VMEM((ne, LANES), F32),
                        pltpu.VMEM((ne, LANES), F32),
                        pltpu.VMEM((tn, tn), BF16)],
        compiler_params=_params(("arbitrary", "arbitrary")),
        name="router",
    )(logits_t, bias_col)


def _idx_copy(idx_hbm, idx_s, sem, tile_i, slot):
    return pltpu.make_async_copy(idx_hbm.at[tile_i], idx_s.at[slot], sem.at[slot])


def _dispatch_kernel(dest_hbm, h_ref, xs_hbm, idx_s, isem, rsem, *, tm):
    i = pl.program_id(0)
    n = pl.num_programs(0)
    slot = i % 2

    @pl.when(i == 0)
    def _():
        _idx_copy(dest_hbm, idx_s, isem, 0, 0).start()

    _idx_copy(dest_hbm, idx_s, isem, i, slot).wait()

    @pl.when(i + 1 < n)
    def _():
        _idx_copy(dest_hbm, idx_s, isem, i + 1, 1 - slot).start()

    def row_copy(t, k):
        f = k * tm + t
        d = idx_s[slot, f >> 7, f & (LANES - 1)]
        return pltpu.make_async_copy(h_ref.at[pl.ds(t, 1)], xs_hbm.at[pl.ds(d, 1)], rsem)

    def start_body(t, c):
        for k in range(TOP_K):
            row_copy(t, k).start()
        return c

    def wait_body(t, c):
        for k in range(TOP_K):
            row_copy(t, k).wait()
        return c

    lax.fori_loop(0, tm, start_body, 0)
    lax.fori_loop(0, tm, wait_body, 0)


def _dispatch(dest_tiles, h2p, n_rows, tm=MOE_TILE):
    t, dw = h2p.shape
    nt = t // tm
    kern = functools.partial(_dispatch_kernel, tm=tm)
    return pl.pallas_call(
        kern,
        grid=(nt,),
        in_specs=[pl.BlockSpec(memory_space=pl.ANY),
                  pl.BlockSpec((tm, dw), lambda i: (i, 0))],
        out_specs=pl.BlockSpec(memory_space=pl.ANY),
        out_shape=jax.ShapeDtypeStruct((n_rows, dw), jnp.uint32),
        scratch_shapes=[pltpu.SMEM((2, TOP_K * tm // LANES, LANES), jnp.int32),
                        pltpu.SemaphoreType.DMA((2,)),
                        pltpu.SemaphoreType.DMA(())],
        compiler_params=_params(("arbitrary",)),
        name="dispatch",
    )(dest_tiles, h2p)


def _moe_kernel(te_ref, tv_ref, xs_ref, wg_ref, wu_ref, wd_ref, ys_ref, wgb, wub, wdb):
    i = pl.program_id(0)
    e = te_ref[i]
    prev = te_ref[jnp.maximum(i - 1, 0)]

    @pl.when((i == 0) | (e != prev))
    def _():
        wgb[...] = wg_ref[...].astype(BF16)
        wub[...] = wu_ref[...].astype(BF16)
        wdb[...] = wd_ref[...].astype(BF16)

    @pl.when(tv_ref[i] > 0)
    def _():
        xa, xb = _unpack_pair(xs_ref[...])
        x = jnp.concatenate([xa, xb], axis=1).astype(BF16)
        a = _silu(_dot(x, wgb[...])) * _dot(x, wub[...])
        y = _dot(a.astype(BF16), wdb[...])
        hw = y.shape[1] // 2
        ys_ref[...] = _pack_pair(y[:, :hw], y[:, hw:])

    @pl.when(tv_ref[i] == 0)
    def _():
        ys_ref[...] = jnp.zeros_like(ys_ref)


def _moe_experts(tile_e, tile_v, xs, wg, wu, wd, tm=MOE_TILE):
    n_rows, dw = xs.shape
    d = 2 * dw
    ff = wg.shape[-1]
    n_tiles = n_rows // tm
    grid_spec = pltpu.PrefetchScalarGridSpec(
        num_scalar_prefetch=2,
        grid=(n_tiles,),
        in_specs=[pl.BlockSpec((tm, dw), lambda i, te, tv: (i, 0)),
                  pl.BlockSpec((None, d, ff), lambda i, te, tv: (te[i], 0, 0)),
                  pl.BlockSpec((None, d, ff), lambda i, te, tv: (te[i], 0, 0)),
                  pl.BlockSpec((None, ff, d), lambda i, te, tv: (te[i], 0, 0))],
        out_specs=pl.BlockSpec((tm, dw), lambda i, te, tv: (i, 0)),
        scratch_shapes=[pltpu.VMEM((d, ff), BF16), pltpu.VMEM((d, ff), BF16), pltpu.VMEM((ff, d), BF16)],
    )
    return pl.pallas_call(
        _moe_kernel,
        grid_spec=grid_spec,
        out_shape=jax.ShapeDtypeStruct((n_rows, dw), jnp.uint32),
        compiler_params=_params(("arbitrary",)),
        name="moe_experts",
    )(tile_e, tile_v, xs, wg, wu, wd)


def _final_kernel(dest_hbm, h2_ref, ys_hbm, wt_ref, x1_ref, mod_ref, pn_ref, wg_ref, wu_ref, wd_ref,
                  o_ref, idx_s, buf, isem, rsem, *, tm):
    i = pl.program_id(0)
    n = pl.num_programs(0)
    slot = i % 2

    @pl.when(i == 0)
    def _():
        _idx_copy(dest_hbm, idx_s, isem, 0, 0).start()

    _idx_copy(dest_hbm, idx_s, isem, i, slot).wait()

    @pl.when(i + 1 < n)
    def _():
        _idx_copy(dest_hbm, idx_s, isem, i + 1, 1 - slot).start()

    def row_copy(t, k):
        f = k * tm + t
        d = idx_s[slot, f >> 7, f & (LANES - 1)]
        return pltpu.make_async_copy(ys_hbm.at[pl.ds(d, 1)], buf.at[k, pl.ds(t, 1)], rsem)

    def start_body(t, c):
        for k in range(TOP_K):
            row_copy(t, k).start()
        return c

    def wait_body(t, c):
        for k in range(TOP_K):
            row_copy(t, k).wait()
        return c

    lax.fori_loop(0, tm, start_body, 0)

    ha, hb = _unpack_pair(h2_ref[...])
    h = jnp.concatenate([ha, hb], axis=1).astype(BF16)
    a = _silu(_dot(h, wg_ref[...])) * _dot(h, wu_ref[...])
    y = _dot(a.astype(BF16), wd_ref[...])

    lax.fori_loop(0, tm, wait_body, 0)

    hw = y.shape[1] // 2
    ya = y[:, :hw]
    yb = y[:, hw:]
    wt = wt_ref[...]
    for k in range(TOP_K):
        ra, rb = _unpack_pair(buf[k])
        wk = wt[:, k:k + 1]
        ya = ya + wk * ra
        yb = yb + wk * rb
    y = jnp.concatenate([ya, yb], axis=1)
    o_ref[...] = x1_ref[...] + mod_ref[5:6, :] * _rms(y, pn_ref[...])


def _final(dest_tiles, h2p, ys, wts_t, x1, mod3, pn, wg, wu, wd, seq, tm=MOE_TILE):
    t, d = x1.shape
    dw = d // 2
    ff = wg.shape[-1]
    tm = min(tm, seq)
    per_b = seq // tm
    kern = functools.partial(_final_kernel, tm=tm)
    return pl.pallas_call(
        kern,
        grid=(t // tm,),
        in_specs=[pl.BlockSpec(memory_space=pl.ANY),
                  pl.BlockSpec((tm, dw), lambda i: (i, 0)),
                  pl.BlockSpec(memory_space=pl.ANY),
                  pl.BlockSpec((tm, TOP_K), lambda i: (i, 0)),
                  pl.BlockSpec((tm, d), lambda i: (i, 0)),
                  pl.BlockSpec((None, 8, d), lambda i: (i // per_b, 0, 0)),
                  pl.BlockSpec((1, d), lambda i: (0, 0)),
                  pl.BlockSpec((d, ff), lambda i: (0, 0)),
                  pl.BlockSpec((d, ff), lambda i: (0, 0)),
                  pl.BlockSpec((ff, d), lambda i: (0, 0))],
        out_specs=pl.BlockSpec((tm, d), lambda i: (i, 0)),
        out_shape=jax.ShapeDtypeStruct((t, d), F32),
        scratch_shapes=[pltpu.SMEM((2, TOP_K * tm // LANES, LANES), jnp.int32),
                        pltpu.VMEM((TOP_K, tm, dw), jnp.uint32),
                        pltpu.SemaphoreType.DMA((2,)),
                        pltpu.SemaphoreType.DMA(())],
        compiler_params=_params(("arbitrary",)),
        name="combine_final",
    )(dest_tiles, h2p, ys, wts_t, x1, mod3, pn, wg, wu, wd)


def kernel(x, c, positions, w_ada, b_ada, pre_norm_mix, post_norm_mix, w_in, conv_w, a_log, dt_bias,
           dn_norm_w, rt_norm_w, w_out, pre_norm_ffn, post_norm_ffn, w_router, router_bias,
           w_gate_exp, w_up_exp, w_down_exp, w_gate_sh, w_up_sh, w_down_sh):
    batch, seq, d = x.shape
    t = batch * seq
    depth = w_ada.shape[0]
    x2d = x.reshape(t, d)
    tm = min(MOE_TILE, seq)

    half = RT_DK // 2
    theta = 1.0 / (ROPE_BASE ** jnp.linspace(0.0, 1.0, half, dtype=F32))
    ang = positions.astype(F32).reshape(t, 1) * theta[None, :]
    cos = jnp.tile(jnp.cos(ang), (1, RT_HEADS))
    sin = jnp.tile(jnp.sin(ang), (1, RT_HEADS))

    c_pad = jnp.zeros((8, d), F32).at[:batch].set(c)
    hd = np.arange(RT_HEADS)[:, None] * RT_DK + np.arange(half)[None, :]
    rot_perm = np.concatenate([hd.reshape(-1), (hd + half).reshape(-1)])

    n_tiles = -(-t * TOP_K // tm) + N_EXPERTS
    n_tmap = -(-n_tiles // LANES) * LANES

    for l in range(depth):
        mod = _ada(c_pad, w_ada[l], b_ada[l].reshape(1, -1))
        mod3 = jnp.zeros((batch, 8, d), F32).at[:, :6].set(mod[:batch].reshape(batch, 6, d))

        wl = w_in[l]
        o = 0
        segs = []
        for wdt in (DN_WIDTH, DN_WIDTH, DN_WIDTH, DN_WIDTH, DN_HEADS, DN_HEADS, RT_QK, RT_QK, RT_WIDTH, RT_WIDTH):
            segs.append(wl[:, o:o + wdt])
            o += wdt
        dq, dk, dv, dz, da, db, rq, rk, rv, rg = segs
        w_main = jnp.concatenate([dq, dk, dv, dz, rq[:, rot_perm], rk[:, rot_perm], rv, rg], axis=1).astype(BF16)
        w_ab = (jnp.zeros((d, 2 * LANES), F32).at[:, :DN_HEADS].set(da)
                .at[:, LANES:LANES + DN_HEADS].set(db)).astype(BF16)

        proj, ab = _inproj(x2d, mod3, pre_norm_mix[l].reshape(1, d), w_main, w_ab, seq)

        alog = jnp.zeros((1, LANES), F32).at[0, :DN_HEADS].set(a_log[l])
        dtb = jnp.zeros((1, LANES), F32).at[0, :DN_HEADS].set(dt_bias[l])
        o_a = _deltanet(proj, ab, conv_w[l], alog, dtb, dn_norm_w[l].reshape(1, DN_DV), batch, seq)
        o_b = _retention(proj, cos, sin, rt_norm_w[l].reshape(1, RT_WIDTH), batch, seq)

        wo = w_out[l].astype(BF16)
        x1, h2p, logits_t = _outproj(o_a, o_b, wo[:DN_WIDTH], wo[DN_WIDTH:], x2d, mod3,
                                     post_norm_mix[l].reshape(1, d), pre_norm_ffn[l].reshape(1, d),
                                     w_router[l].T, seq)

        dest, wts, tmap = _route(logits_t, router_bias[l].reshape(N_EXPERTS, 1), n_tmap, tile=tm)
        dest_tiles = (dest.reshape(TOP_K, t // tm, tm).transpose(1, 0, 2)
                      .reshape(t // tm, TOP_K * tm // LANES, LANES))
        xs = _dispatch(dest_tiles, h2p, n_tiles * tm, tm=tm)
        ys = _moe_experts(tmap[0, :n_tiles], tmap[1, :n_tiles], xs,
                          w_gate_exp[l], w_up_exp[l], w_down_exp[l], tm=tm)

        x2d = _final(dest_tiles, h2p, ys, wts.T, x1, mod3, post_norm_ffn[l].reshape(1, d),
                     w_gate_sh[l].astype(BF16), w_up_sh[l].astype(BF16), w_down_sh[l].astype(BF16), seq, tm=tm)
    return x2d.reshape(batch, seq, d)
```

```python
import functools
import math

import numpy as np
import jax
import jax.numpy as jnp
from jax import lax
from jax.experimental import pallas as pl
from jax.experimental.pallas import tpu as pltpu

F32 = jnp.float32
BF16 = jnp.bfloat16

D_MODEL = 2048
DN_HEADS = 8
DN_DK = 128
DN_DV = 128
DN_CONV = 4
DN_CHUNK = 64
RT_HEADS = 8
RT_DK = 64
RT_DV = 128
ROPE_BASE = 10000.0
DN_WIDTH = DN_HEADS * DN_DV
RT_WIDTH = RT_HEADS * RT_DV
RT_QK = RT_HEADS * RT_DK
N_EXPERTS = 64
TOP_K = 8
N_GROUPS = 8
TOPK_GROUPS = 4
EXPERT_FF = 512
SHARED_FF = 512
ROUTED_SCALE = 2.5
NORM_EPS = 1e-6

LANES = 128
VMEM_LIMIT = 56 * 1024 * 1024

PROJ_WIDTH = 3 * DN_WIDTH + DN_WIDTH + 2 * RT_QK + 2 * RT_WIDTH
DN_ROWS = 256
RT_CHUNK = 256
MOE_TILE = 256


def _silu(x):
    return x * jax.nn.sigmoid(x)


def _softplus(x):
    return jnp.maximum(x, 0.0) + jnp.log1p(jnp.exp(-jnp.abs(x)))


def _dot(a, b, precision=None):
    return jnp.dot(a, b, preferred_element_type=F32, precision=precision)


def _dot_nt(a, b, precision=None):
    return lax.dot_general(a, b, (((1,), (1,)), ((), ())), preferred_element_type=F32, precision=precision)


def _dot_tn(a, b, precision=None):
    return lax.dot_general(a, b, (((0,), (0,)), ((), ())), preferred_element_type=F32, precision=precision)


def _rms(x, w):
    return x * lax.rsqrt(jnp.mean(x * x, axis=-1, keepdims=True) + NORM_EPS) * w


def _params(sem):
    return pltpu.CompilerParams(dimension_semantics=sem, vmem_limit_bytes=VMEM_LIMIT)


def _ada_kernel(c_ref, w_ref, b_ref, o_ref):
    ca = _silu(c_ref[...])
    o_ref[...] = _dot(ca.astype(BF16), w_ref[...].astype(BF16)) + b_ref[...]


def _ada(c_pad, w_ada, b_ada, tn=1024):
    m, k = c_pad.shape
    n = w_ada.shape[1]
    return pl.pallas_call(
        _ada_kernel,
        grid=(n // tn,),
        in_specs=[pl.BlockSpec((m, k), lambda j: (0, 0)),
                  pl.BlockSpec((k, tn), lambda j: (0, j)),
                  pl.BlockSpec((1, tn), lambda j: (0, j))],
        out_specs=pl.BlockSpec((m, tn), lambda j: (0, j)),
        out_shape=jax.ShapeDtypeStruct((m, n), F32),
        compiler_params=_params(("arbitrary",)),
        name="ada_mod",
    )(c_pad, w_ada, b_ada)


def _inproj_kernel(x_ref, mod_ref, nw_ref, w_ref, wab_ref, o_ref, ab_ref, h_ref):
    j = pl.program_id(1)

    @pl.when(j == 0)
    def _():
        x = x_ref[...]
        h = _rms(x, nw_ref[...]) * (1.0 + mod_ref[1:2, :]) + mod_ref[0:1, :]
        hb = h.astype(BF16)
        h_ref[...] = hb
        ab_ref[...] = _dot(hb, wab_ref[...])

    o_ref[...] = _dot(h_ref[...], w_ref[...]).astype(o_ref.dtype)


def _inproj(x2d, mod3, nw, w_main, w_ab, seq, tm=1024, tn=1024):
    t, d = x2d.shape
    n = w_main.shape[1]
    tm = min(tm, seq)
    per_b = seq // tm
    return pl.pallas_call(
        _inproj_kernel,
        grid=(t // tm, n // tn),
        in_specs=[pl.BlockSpec((tm, d), lambda i, j: (i, 0)),
                  pl.BlockSpec((None, 8, d), lambda i, j: (i // per_b, 0, 0)),
                  pl.BlockSpec((1, d), lambda i, j: (0, 0)),
                  pl.BlockSpec((d, tn), lambda i, j: (0, j)),
                  pl.BlockSpec((d, 2 * LANES), lambda i, j: (0, 0))],
        out_specs=[pl.BlockSpec((tm, tn), lambda i, j: (i, j)),
                   pl.BlockSpec((tm, 2 * LANES), lambda i, j: (i, 0))],
        out_shape=[jax.ShapeDtypeStruct((t, n), BF16),
                   jax.ShapeDtypeStruct((t, 2 * LANES), F32)],
        scratch_shapes=[pltpu.VMEM((tm, d), BF16)],
        compiler_params=_params(("arbitrary", "arbitrary")),
        name="in_proj",
    )(x2d, mod3, nw, w_main, w_ab)


def _dn_kernel(qkv_ref, z_ref, ab_ref, cw_ref, alog_ref, dtb_ref, nw_ref, o_ref,
               ext_ref, q_s, k_s, v_s, g_s, b_s, st_ref, *, rows, chunk):
    n = pl.program_id(1)
    w3 = 3 * DN_WIDTH

    @pl.when(n == 0)
    def _():
        ext_ref[0:8, :] = jnp.zeros((8, w3), F32)
        st_ref[...] = jnp.zeros_like(st_ref)

    x = qkv_ref[...].astype(F32)
    ext_ref[8:8 + rows, :] = x
    cw = cw_ref[...]
    acc = x * cw[3:4, :]
    for j in range(DN_CONV - 1):
        acc = acc + ext_ref[5 + j:5 + j + rows, :] * cw[j:j + 1, :]
    ext_ref[0:8, :] = x[rows - 8:rows, :]
    act = _silu(acc)

    scale = DN_DK ** -0.5
    for h in range(DN_HEADS):
        sl = slice(h * DN_DK, (h + 1) * DN_DK)
        qh = act[:, sl]
        kh = act[:, DN_WIDTH + h * DN_DK:DN_WIDTH + (h + 1) * DN_DK]
        q_s[:, sl] = qh * (lax.rsqrt(jnp.sum(qh * qh, axis=-1, keepdims=True) + NORM_EPS) * scale)
        k_s[:, sl] = kh * lax.rsqrt(jnp.sum(kh * kh, axis=-1, keepdims=True) + NORM_EPS)
    v_s[...] = act[:, 2 * DN_WIDTH:]

    ab = ab_ref[...]
    g_s[...] = -jnp.exp(alog_ref[...]) * _softplus(ab[:, :LANES] + dtb_ref[...])
    b_s[...] = jax.nn.sigmoid(ab[:, LANES:])

    c = chunk
    ri = lax.broadcasted_iota(jnp.int32, (c, c), 0)
    ci = lax.broadcasted_iota(jnp.int32, (c, c), 1)
    causal = ri >= ci
    strict = ri > ci
    ltri = causal.astype(F32)
    utri = (ri <= ci).astype(F32)
    eye = (ri == ci).astype(F32)
    nw = nw_ref[...]
    hp = lax.Precision.HIGHEST

    def chunk_body(ic, carry):
        r0 = pl.multiple_of(ic * c, c)
        g_c = g_s[pl.ds(r0, c), :]
        b_c = b_s[pl.ds(r0, c), :]
        gc = _dot(ltri, g_c, hp)
        gct = _dot_tn(g_c, utri, hp)
        glast = gc[c - 1:c, :]
        eg = jnp.exp(gc)
        ekd = jnp.exp(glast - gc)
        cdv = jnp.exp(glast)
        hs = range(DN_HEADS)
        sls = [slice(h * DN_DK, (h + 1) * DN_DK) for h in hs]
        q = [q_s[pl.ds(r0, c), sls[h]] for h in hs]
        k = [k_s[pl.ds(r0, c), sls[h]] for h in hs]
        v = [v_s[pl.ds(r0, c), sls[h]] for h in hs]
        decay = [jnp.where(causal, jnp.exp(jnp.where(causal, gc[:, h:h + 1] - gct[h:h + 1, :], 0.0)), 0.0)
                 for h in hs]
        bcol = [b_c[:, h:h + 1] for h in hs]
        kb = [k[h] * bcol[h] for h in hs]
        a_mat = [jnp.where(strict, _dot_nt(kb[h], k[h]) * decay[h], 0.0) for h in hs]
        attn = [jnp.where(causal, _dot_nt(q[h], k[h]) * decay[h], 0.0) for h in hs]
        xi = [eye - a_mat[h] for h in hs]
        pw = [_dot(a_mat[h], a_mat[h]) for h in hs]
        kk = 2
        while kk < c:
            xi = [xi[h] + _dot(xi[h], pw[h]) for h in hs]
            kk *= 2
            if kk < c:
                pw = [_dot(pw[h], pw[h]) for h in hs]
        rhs = [jnp.concatenate([kb[h] * eg[:, h:h + 1], v[h] * bcol[h]], axis=1) for h in hs]
        sol = [_dot(xi[h], rhs[h]) for h in hs]
        s = [st_ref[h] for h in hs]
        v_new = [sol[h][:, DN_DK:] - _dot(sol[h][:, :DN_DK], s[h]) for h in hs]
        out = [_dot(q[h] * eg[:, h:h + 1], s[h]) + _dot(attn[h], v_new[h]) for h in hs]
        for h in hs:
            st_ref[h] = s[h] * cdv[:, h:h + 1] + _dot_tn(k[h] * ekd[:, h:h + 1], v_new[h])
        for h in hs:
            zz = z_ref[pl.ds(r0, c), sls[h]].astype(F32)
            o_ref[pl.ds(r0, c), sls[h]] = (_rms(out[h], nw) * _silu(zz)).astype(o_ref.dtype)
        return carry

    lax.fori_loop(0, rows // c, chunk_body, 0)


def _deltanet(proj, ab, conv_w, alog, dtb, nw, batch, seq, rows=DN_ROWS, chunk=DN_CHUNK):
    rows = min(rows, seq)
    nb = seq // rows
    t = batch * seq
    w3 = 3 * DN_WIDTH
    kern = functools.partial(_dn_kernel, rows=rows, chunk=chunk)
    return pl.pallas_call(
        kern,
        grid=(batch, nb),
        in_specs=[pl.BlockSpec((rows, w3), lambda b, n: (b * nb + n, 0)),
                  pl.BlockSpec((rows, DN_WIDTH), lambda b, n: (b * nb + n, 3)),
                  pl.BlockSpec((rows, 2 * LANES), lambda b, n: (b * nb + n, 0)),
                  pl.BlockSpec((DN_CONV, w3), lambda b, n: (0, 0)),
                  pl.BlockSpec((1, LANES), lambda b, n: (0, 0)),
                  pl.BlockSpec((1, LANES), lambda b, n: (0, 0)),
                  pl.BlockSpec((1, DN_DV), lambda b, n: (0, 0))],
        out_specs=pl.BlockSpec((rows, DN_WIDTH), lambda b, n: (b * nb + n, 0)),
        out_shape=jax.ShapeDtypeStruct((t, DN_WIDTH), BF16),
        scratch_shapes=[pltpu.VMEM((rows + 8, w3), F32),
                        pltpu.VMEM((rows, DN_WIDTH), F32),
                        pltpu.VMEM((rows, DN_WIDTH), F32),
                        pltpu.VMEM((rows, DN_WIDTH), F32),
                        pltpu.VMEM((rows, LANES), F32),
                        pltpu.VMEM((rows, LANES), F32),
                        pltpu.VMEM((DN_HEADS, DN_DK, DN_DV), F32)],
        compiler_params=_params(("arbitrary", "arbitrary")),
        name="deltanet",
    )(proj, proj, ab, conv_w, alog, dtb, nw)


def _rt_log_gamma(idx):
    return jnp.log(1.0 - jnp.exp2(-5.0 - idx.astype(F32)))


def _rt_kernel(qk_ref, v_ref, g_ref, cos_ref, sin_ref, nw_ref, o_ref, st_ref, msk_ref, *, chunk):
    n = pl.program_id(1)
    c = chunk
    half = RT_QK // 2

    @pl.when(n == 0)
    def _():
        st_ref[...] = jnp.zeros_like(st_ref)
        rh = (lax.broadcasted_iota(jnp.int32, (RT_QK, RT_WIDTH), 0) & (half - 1)) >> 5
        ch = lax.broadcasted_iota(jnp.int32, (RT_QK, RT_WIDTH), 1) >> 7
        msk_ref[...] = jnp.where(rh == ch, jnp.exp(float(c) * _rt_log_gamma(rh)), 0.0)

    cos = cos_ref[...]
    sin = sin_ref[...]
    qk = qk_ref[...].astype(F32)
    q1, q2 = qk[:, 0:half], qk[:, half:2 * half]
    k1, k2 = qk[:, 2 * half:3 * half], qk[:, 3 * half:4 * half]
    kscale = RT_DK ** -0.5
    q = jnp.concatenate([q1 * cos - q2 * sin, q2 * cos + q1 * sin], axis=1)
    k = jnp.concatenate([k1 * cos - k2 * sin, k2 * cos + k1 * sin], axis=1) * kscale
    v = v_ref[...].astype(F32)

    lane_h = (lax.broadcasted_iota(jnp.int32, (1, RT_QK), 1) & (half - 1)) >> 5
    lg_lane = _rt_log_gamma(lane_h)
    row = lax.broadcasted_iota(jnp.int32, (c, 1), 0).astype(F32)
    q_dec = q * jnp.exp((row + 1.0) * lg_lane)
    k_dec = k * jnp.exp((float(c) - 1.0 - row) * lg_lane)

    ri = lax.broadcasted_iota(jnp.int32, (c, c), 0)
    ci = lax.broadcasted_iota(jnp.int32, (c, c), 1)
    causal = ri >= ci
    rel = jnp.where(causal, (ri - ci).astype(F32), 0.0)

    s_all = st_ref[...]
    inter = _dot(q_dec, s_all)
    msk = msk_ref[...]
    st_ref[...] = s_all * msk + jnp.where(msk > 0.0, _dot_tn(k_dec, v), 0.0)

    nw = nw_ref[...]
    for h in range(RT_HEADS):
        sl = slice(h * RT_DV, (h + 1) * RT_DV)
        lg = math.log(1.0 - 2.0 ** (-5.0 - h))
        d_h = jnp.where(causal, jnp.exp(rel * lg), 0.0)
        qm = jnp.where(lane_h == h, q, 0.0)
        sc = _dot_nt(qm, k) * d_h
        o = _dot(sc, v[:, sl]) + inter[:, sl]
        mu = jnp.mean(o, axis=-1, keepdims=True)
        var = jnp.mean(jnp.square(o - mu), axis=-1, keepdims=True)
        y = (o - mu) * lax.rsqrt(var + NORM_EPS) * nw[:, sl]
        gg = g_ref[:, sl].astype(F32)
        o_ref[:, sl] = (y * _silu(gg)).astype(o_ref.dtype)


def _retention(proj, cos, sin, nw, batch, seq, chunk=RT_CHUNK):
    chunk = min(chunk, seq)
    nb = seq // chunk
    t = batch * seq
    kern = functools.partial(_rt_kernel, chunk=chunk)
    return pl.pallas_call(
        kern,
        grid=(batch, nb),
        in_specs=[pl.BlockSpec((chunk, 2 * RT_QK), lambda b, n: (b * nb + n, 4)),
                  pl.BlockSpec((chunk, RT_WIDTH), lambda b, n: (b * nb + n, 5)),
                  pl.BlockSpec((chunk, RT_WIDTH), lambda b, n: (b * nb + n, 6)),
                  pl.BlockSpec((chunk, RT_QK // 2), lambda b, n: (b * nb + n, 0)),
                  pl.BlockSpec((chunk, RT_QK // 2), lambda b, n: (b * nb + n, 0)),
                  pl.BlockSpec((1, RT_WIDTH), lambda b, n: (0, 0))],
        out_specs=pl.BlockSpec((chunk, RT_WIDTH), lambda b, n: (b * nb + n, 0)),
        out_shape=jax.ShapeDtypeStruct((t, RT_WIDTH), BF16),
        scratch_shapes=[pltpu.VMEM((RT_QK, RT_WIDTH), F32),
                        pltpu.VMEM((RT_QK, RT_WIDTH), F32)],
        compiler_params=_params(("arbitrary", "arbitrary")),
        name="retention",
    )(proj, proj, proj, cos, sin, nw)


def _pack_pair(a, b):
    au = lax.bitcast_convert_type(a.astype(BF16).astype(F32), jnp.uint32)
    bu = lax.bitcast_convert_type(b.astype(BF16).astype(F32), jnp.uint32)
    return (au >> 16) | bu


def _unpack_pair(w):
    a = lax.bitcast_convert_type(w << 16, F32)
    b = lax.bitcast_convert_type(w & jnp.uint32(0xFFFF0000), F32)
    return a, b


def _outproj_kernel(oa_ref, ob_ref, wa_ref, wb_ref, x_ref, mod_ref, pn_ref, fn_ref, wr_ref,
                    x1_ref, h2_ref, lg_ref):
    y = _dot(oa_ref[...], wa_ref[...]) + _dot(ob_ref[...], wb_ref[...])
    x1 = x_ref[...] + mod_ref[2:3, :] * _rms(y, pn_ref[...])
    x1_ref[...] = x1
    h2 = _rms(x1, fn_ref[...]) * (1.0 + mod_ref[4:5, :]) + mod_ref[3:4, :]
    hw = h2.shape[1] // 2
    h2_ref[...] = _pack_pair(h2[:, :hw], h2[:, hw:])
    h_hi = h2.astype(BF16)
    h_lo = (h2 - h_hi.astype(F32)).astype(BF16)
    wr = wr_ref[...]
    w_hi = wr.astype(BF16)
    w_lo = (wr - w_hi.astype(F32)).astype(BF16)
    lg_ref[...] = _dot_nt(w_hi, h_hi) + (_dot_nt(w_lo, h_hi) + _dot_nt(w_hi, h_lo))


def _outproj(o_a, o_b, wa, wb, x2d, mod3, pn, fn, wr_t, seq, tm=512):
    t, d = x2d.shape
    tm = min(tm, seq)
    per_b = seq // tm
    return pl.pallas_call(
        _outproj_kernel,
        grid=(t // tm,),
        in_specs=[pl.BlockSpec((tm, DN_WIDTH), lambda i: (i, 0)),
                  pl.BlockSpec((tm, RT_WIDTH), lambda i: (i, 0)),
                  pl.BlockSpec((DN_WIDTH, d), lambda i: (0, 0)),
                  pl.BlockSpec((RT_WIDTH, d), lambda i: (0, 0)),
                  pl.BlockSpec((tm, d), lambda i: (i, 0)),
                  pl.BlockSpec((None, 8, d), lambda i: (i // per_b, 0, 0)),
                  pl.BlockSpec((1, d), lambda i: (0, 0)),
                  pl.BlockSpec((1, d), lambda i: (0, 0)),
                  pl.BlockSpec((N_EXPERTS, d), lambda i: (0, 0))],
        out_specs=[pl.BlockSpec((tm, d), lambda i: (i, 0)),
                   pl.BlockSpec((tm, d // 2), lambda i: (i, 0)),
                   pl.BlockSpec((N_EXPERTS, tm), lambda i: (0, i))],
        out_shape=[jax.ShapeDtypeStruct((t, d), F32),
                   jax.ShapeDtypeStruct((t, d // 2), jnp.uint32),
                   jax.ShapeDtypeStruct((N_EXPERTS, t), F32)],
        compiler_params=_params(("arbitrary",)),
        name="out_proj",
    )(o_a, o_b, wa, wb, x2d, mod3, pn, fn, wr_t)


def _allreduce8(x, op):
    for sh in (1, 2, 4):
        x = op(x, pltpu.roll(x, sh, 0))
    return x


def _route_kernel(lg_ref, bias_ref, dest_ref, wts_ref, tmap_ref, cnt_s, base_s, ust_s, *, tn, tile):
    p = pl.program_id(0)
    j = pl.program_id(1)
    ng = N_GROUPS
    gs = N_EXPERTS // N_GROUPS
    n_tmap = tmap_ref.shape[1]
    hp = lax.Precision.HIGHEST

    @pl.when((p == 0) & (j == 0))
    def _():
        cnt_s[...] = jnp.zeros_like(cnt_s)
        base_s[...] = jnp.zeros_like(base_s)
        tmap_ref[...] = jnp.zeros_like(tmap_ref)
        r = lax.broadcasted_iota(jnp.int32, (tn, tn), 0)
        c = lax.broadcasted_iota(jnp.int32, (tn, tn), 1)
        ust_s[...] = jnp.where(r < c, 1.0, 0.0).astype(BF16)

    @pl.when((p == 1) & (j == 0))
    def _():
        cnt = cnt_s[...]
        padded = jnp.floor((cnt + float(tile - 1)) * (1.0 / tile)) * float(tile)
        ri = lax.broadcasted_iota(jnp.int32, (N_EXPERTS, N_EXPERTS), 0)
        ci = lax.broadcasted_iota(jnp.int32, (N_EXPERTS, N_EXPERTS), 1)
        pad_end = _dot(jnp.where(ri >= ci, 1.0, 0.0), padded, hp)
        base_s[...] = pad_end - padded
        cnt_s[...] = jnp.zeros_like(cnt_s)
        ts = lax.broadcasted_iota(jnp.int32, (N_EXPERTS, n_tmap), 1).astype(F32) * float(tile)
        ended = jnp.where(pad_end[:, 0:1] <= ts, 1.0, 0.0)
        te = jnp.minimum(jnp.sum(ended, axis=0, keepdims=True), float(N_EXPERTS - 1))
        total = pad_end[N_EXPERTS - 1:N_EXPERTS, 0:1]
        tv = jnp.where(ts[0:1, :] < total, 1, 0)
        diag = (lax.broadcasted_iota(jnp.int32, (N_EXPERTS, LANES), 0)
                == lax.broadcasted_iota(jnp.int32, (N_EXPERTS, LANES), 1))
        zpad = jnp.zeros((1, n_tmap - LANES), F32)
        cnt_row = jnp.concatenate([jnp.sum(jnp.where(diag, cnt, 0.0), axis=0, keepdims=True), zpad], axis=1)
        ps_row = jnp.concatenate([jnp.sum(jnp.where(diag, pad_end - padded, 0.0), axis=0, keepdims=True), zpad],
                                 axis=1)
        row = lax.broadcasted_iota(jnp.int32, (8, n_tmap), 0)
        tmap_ref[...] = jnp.where(row == 0, te.astype(jnp.int32),
                                  jnp.where(row == 1, tv,
                                            jnp.where(row == 2, cnt_row.astype(jnp.int32),
                                                      jnp.where(row == 3, ps_row.astype(jnp.int32), 0))))

    s_all = jax.nn.sigmoid(lg_ref[...])
    sel_all = s_all + bias_ref[...]
    sub = lax.broadcasted_iota(jnp.int32, (gs, tn), 0)
    neg = -jnp.inf

    vs, ss, gsc = [], [], []
    for g in range(ng):
        v = sel_all[g * gs:(g + 1) * gs, :]
        m1 = _allreduce8(v, jnp.maximum)
        first = _allreduce8(jnp.where(v == m1, sub, gs), jnp.minimum)
        m2 = _allreduce8(jnp.where(sub == first, neg, v), jnp.maximum)
        vs.append(v)
        ss.append(s_all[g * gs:(g + 1) * gs, :])
        gsc.append(m1 + m2)
    vm = []
    for g in range(ng):
        ahead = jnp.zeros((gs, tn), jnp.int32)
        for g2 in range(ng):
            if g2 == g:
                continue
            b = (gsc[g2] >= gsc[g]) if g2 < g else (gsc[g2] > gsc[g])
            ahead = ahead + jnp.where(b, 1, 0)
        vm.append(jnp.where(ahead < TOPK_GROUPS, vs[g], neg))
    firsts = []
    mem = [jnp.zeros((gs, tn), F32) for _ in range(ng)]
    w_rows = jnp.zeros((gs, tn), F32)
    for k in range(TOP_K):
        mx = vm[0]
        for g in range(1, ng):
            mx = jnp.maximum(mx, vm[g])
        mx = _allreduce8(mx, jnp.maximum)
        cand = jnp.full((gs, tn), N_EXPERTS, jnp.int32)
        for g in range(ng):
            cand = jnp.minimum(cand, jnp.where(vm[g] == mx, sub + g * gs, N_EXPERTS))
        first = _allreduce8(cand, jnp.minimum)
        wk = jnp.zeros((gs, tn), F32)
        for g in range(ng):
            hit = (sub + g * gs) == first
            vm[g] = jnp.where(hit, neg, vm[g])
            wk = wk + jnp.where(hit, ss[g], 0.0)
            mem[g] = mem[g] + jnp.where(hit, 1.0, 0.0)
        wk = _allreduce8(wk, jnp.add)
        w_rows = jnp.where(sub == k, wk, w_rows)
        firsts.append(first)
    wsum = _allreduce8(w_rows, jnp.add)
    wts_ref[...] = w_rows / wsum * ROUTED_SCALE

    m_all = jnp.concatenate(mem, axis=0)
    pref = _dot(m_all.astype(BF16), ust_s[...])
    tot = pref + (base_s[...] + cnt_s[...])[:, 0:1]
    d_rows = jnp.zeros((gs, tn), F32)
    for k in range(TOP_K):
        dk = jnp.zeros((gs, tn), F32)
        for g in range(ng):
            dk = dk + jnp.where((sub + g * gs) == firsts[k], tot[g * gs:(g + 1) * gs, :], 0.0)
        d_rows = jnp.where(sub == k, _allreduce8(dk, jnp.add), d_rows)
    dest_ref[...] = d_rows.astype(jnp.int32)
    cnt_s[...] = cnt_s[...] + jnp.sum(m_all, axis=1, keepdims=True)


def _route(logits_t, bias_col, n_tmap, tn=512, tile=MOE_TILE):
    ne, t = logits_t.shape
    tn = min(tn, t)
    nb = t // tn
    kern = functools.partial(_route_kernel, tn=tn, tile=tile)
    return pl.pallas_call(
        kern,
        grid=(2, nb),
        in_specs=[pl.BlockSpec((ne, tn), lambda p, j: (0, j)),
                  pl.BlockSpec((ne, 1), lambda p, j: (0, 0))],
        out_specs=[pl.BlockSpec((TOP_K, tn), lambda p, j: (0, j * p)),
                   pl.BlockSpec((TOP_K, tn), lambda p, j: (0, j * p)),
                   pl.BlockSpec((8, n_tmap), lambda p, j: (0, 0))],
        out_shape=[jax.ShapeDtypeStruct((TOP_K, t), jnp.int32),
                   jax.ShapeDtypeStruct((TOP_K, t), F32),
                   jax.ShapeDtypeStruct((8, n_tmap), jnp.int32)],
        scratch_shapes=[pltpu.VMEM((ne, LANES), F32),
                        pltpu.VMEM((ne, LANES), F32),
                        pltpu.VMEM((tn, tn), BF16)],
        compiler_params=_params(("arbitrary", "arbitrary")),
        name="router",
    )(logits_t, bias_col)


def _idx_copy(idx_hbm, idx_s, sem, tile_i, slot):
    return pltpu.make_async_copy(idx_hbm.at[tile_i], idx_s.at[slot], sem.at[slot])


def _dispatch_kernel(tmap_ref, dest_hbm, h_ref, xs_hbm, idx_s, zbuf, isem, rsem, zsem, *, tm, n_tiles):
    i = pl.program_id(0)
    n = pl.num_programs(0)
    slot = i % 2
    sh = tm.bit_length() - 1

    def pad_copy(r):
        return pltpu.make_async_copy(zbuf.at[pl.ds(0, 1)], xs_hbm.at[pl.ds(r, 1)], zsem)

    def tail_copy(j):
        return pltpu.make_async_copy(zbuf, xs_hbm.at[pl.ds(pl.multiple_of(j * tm, tm), tm)], zsem)

    def pad_range(e):
        cnt = tmap_ref[2, e]
        ps = tmap_ref[3, e]
        return ps + cnt, ps + (((cnt + (tm - 1)) >> sh) << sh)

    def zero_fill(wait):
        def per_row(r, c):
            pad_copy(r).wait() if wait else pad_copy(r).start()
            return c

        def per_expert(e, c):
            lo, hi = pad_range(e)
            return lax.fori_loop(lo, hi, per_row, c)

        def per_tile(j, c):
            tail_copy(j).wait() if wait else tail_copy(j).start()
            return c

        lax.fori_loop(0, N_EXPERTS, per_expert, 0)
        lax.fori_loop(pad_range(N_EXPERTS - 1)[1] >> sh, n_tiles, per_tile, 0)

    @pl.when(i == 0)
    def _():
        zbuf[...] = jnp.zeros_like(zbuf)
        zero_fill(False)
        _idx_copy(dest_hbm, idx_s, isem, 0, 0).start()

    _idx_copy(dest_hbm, idx_s, isem, i, slot).wait()

    @pl.when(i + 1 < n)
    def _():
        _idx_copy(dest_hbm, idx_s, isem, i + 1, 1 - slot).start()

    def row_copy(t, k):
        f = k * tm + t
        d = idx_s[slot, f >> 7, f & (LANES - 1)]
        return pltpu.make_async_copy(h_ref.at[pl.ds(t, 1)], xs_hbm.at[pl.ds(d, 1)], rsem)

    def start_body(t, c):
        for k in range(TOP_K):
            row_copy(t, k).start()
        return c

    def wait_body(t, c):
        for k in range(TOP_K):
            row_copy(t, k).wait()
        return c

    lax.fori_loop(0, tm, start_body, 0)
    lax.fori_loop(0, tm, wait_body, 0)

    @pl.when(i == 0)
    def _():
        zero_fill(True)


def _dispatch(tmap, dest_tiles, h2p, n_tiles, tm=MOE_TILE):
    t, dw = h2p.shape
    nt = t // tm
    kern = functools.partial(_dispatch_kernel, tm=tm, n_tiles=n_tiles)
    grid_spec = pltpu.PrefetchScalarGridSpec(
        num_scalar_prefetch=1,
        grid=(nt,),
        in_specs=[pl.BlockSpec(memory_space=pl.ANY),
                  pl.BlockSpec((tm, dw), lambda i, tmap: (i, 0))],
        out_specs=pl.BlockSpec(memory_space=pl.ANY),
        scratch_shapes=[pltpu.SMEM((2, TOP_K * tm // LANES, LANES), jnp.int32),
                        pltpu.VMEM((tm, dw), jnp.uint32),
                        pltpu.SemaphoreType.DMA((2,)),
                        pltpu.SemaphoreType.DMA(()),
                        pltpu.SemaphoreType.DMA(())],
    )
    return pl.pallas_call(
        kern,
        grid_spec=grid_spec,
        out_shape=jax.ShapeDtypeStruct((n_tiles * tm, dw), jnp.uint32),
        compiler_params=_params(("arbitrary",)),
        name="dispatch",
    )(tmap, dest_tiles, h2p)


def _moe_kernel(te_ref, tv_ref, xs_ref, wg_ref, wu_ref, wd_ref, ys_ref, wgb, wub, wdb):
    i = pl.program_id(0)
    e = te_ref[i]
    prev = te_ref[jnp.maximum(i - 1, 0)]

    @pl.when((i == 0) | (e != prev))
    def _():
        wgb[...] = wg_ref[...].astype(BF16)
        wub[...] = wu_ref[...].astype(BF16)
        wdb[...] = wd_ref[...].astype(BF16)

    @pl.when(tv_ref[i] > 0)
    def _():
        xa, xb = _unpack_pair(xs_ref[...])
        x = jnp.concatenate([xa, xb], axis=1).astype(BF16)
        a = _silu(_dot(x, wgb[...])) * _dot(x, wub[...])
        y = _dot(a.astype(BF16), wdb[...])
        hw = y.shape[1] // 2
        ys_ref[...] = _pack_pair(y[:, :hw], y[:, hw:])

    @pl.when(tv_ref[i] == 0)
    def _():
        ys_ref[...] = jnp.zeros_like(ys_ref)


def _moe_experts(tile_e, tile_v, xs, wg, wu, wd, tm=MOE_TILE):
    n_rows, dw = xs.shape
    d = 2 * dw
    ff = wg.shape[-1]
    n_tiles = n_rows // tm
    grid_spec = pltpu.PrefetchScalarGridSpec(
        num_scalar_prefetch=2,
        grid=(n_tiles,),
        in_specs=[pl.BlockSpec((tm, dw), lambda i, te, tv: (i, 0)),
                  pl.BlockSpec((None, d, ff), lambda i, te, tv: (te[i], 0, 0)),
                  pl.BlockSpec((None, d, ff), lambda i, te, tv: (te[i], 0, 0)),
                  pl.BlockSpec((None, ff, d), lambda i, te, tv: (te[i], 0, 0))],
        out_specs=pl.BlockSpec((tm, dw), lambda i, te, tv: (i, 0)),
        scratch_shapes=[pltpu.VMEM((d, ff), BF16), pltpu.VMEM((d, ff), BF16), pltpu.VMEM((ff, d), BF16)],
    )
    return pl.pallas_call(
        _moe_kernel,
        grid_spec=grid_spec,
        out_shape=jax.ShapeDtypeStruct((n_rows, dw), jnp.uint32),
        compiler_params=_params(("arbitrary",)),
        name="moe_experts",
    )(tile_e, tile_v, xs, wg, wu, wd)


def _final_kernel(dest_hbm, h2_ref, ys_hbm, wt_ref, x1_ref, mod_ref, pn_ref, wg_ref, wu_ref, wd_ref,
                  o_ref, idx_s, buf, isem, rsem, *, tm):
    i = pl.program_id(0)
    n = pl.num_programs(0)
    slot = i % 2

    @pl.when(i == 0)
    def _():
        _idx_copy(dest_hbm, idx_s, isem, 0, 0).start()

    _idx_copy(dest_hbm, idx_s, isem, i, slot).wait()

    @pl.when(i + 1 < n)
    def _():
        _idx_copy(dest_hbm, idx_s, isem, i + 1, 1 - slot).start()

    def row_copy(t, k):
        f = k * tm + t
        d = idx_s[slot, f >> 7, f & (LANES - 1)]
        return pltpu.make_async_copy(ys_hbm.at[pl.ds(d, 1)], buf.at[k, pl.ds(t, 1)], rsem)

    def start_body(t, c):
        for k in range(TOP_K):
            row_copy(t, k).start()
        return c

    def wait_body(t, c):
        for k in range(TOP_K):
            row_copy(t, k).wait()
        return c

    lax.fori_loop(0, tm, start_body, 0)

    ha, hb = _unpack_pair(h2_ref[...])
    h = jnp.concatenate([ha, hb], axis=1).astype(BF16)
    a = _silu(_dot(h, wg_ref[...])) * _dot(h, wu_ref[...])
    y = _dot(a.astype(BF16), wd_ref[...])

    lax.fori_loop(0, tm, wait_body, 0)

    hw = y.shape[1] // 2
    ya = y[:, :hw]
    yb = y[:, hw:]
    wt = wt_ref[...]
    for k in range(TOP_K):
        ra, rb = _unpack_pair(buf[k])
        wk = wt[:, k:k + 1]
        ya = ya + wk * ra
        yb = yb + wk * rb
    y = jnp.concatenate([ya, yb], axis=1)
    o_ref[...] = x1_ref[...] + mod_ref[5:6, :] * _rms(y, pn_ref[...])


def _final(dest_tiles, h2p, ys, wts_t, x1, mod3, pn, wg, wu, wd, seq, tm=MOE_TILE):
    t, d = x1.shape
    dw = d // 2
    ff = wg.shape[-1]
    tm = min(tm, seq)
    per_b = seq // tm
    kern = functools.partial(_final_kernel, tm=tm)
    return pl.pallas_call(
        kern,
        grid=(t // tm,),
        in_specs=[pl.BlockSpec(memory_space=pl.ANY),
                  pl.BlockSpec((tm, dw), lambda i: (i, 0)),
                  pl.BlockSpec(memory_space=pl.ANY),
                  pl.BlockSpec((tm, TOP_K), lambda i: (i, 0)),
                  pl.BlockSpec((tm, d), lambda i: (i, 0)),
                  pl.BlockSpec((None, 8, d), lambda i: (i // per_b, 0, 0)),
                  pl.BlockSpec((1, d), lambda i: (0, 0)),
                  pl.BlockSpec((d, ff), lambda i: (0, 0)),
                  pl.BlockSpec((d, ff), lambda i: (0, 0)),
                  pl.BlockSpec((ff, d), lambda i: (0, 0))],
        out_specs=pl.BlockSpec((tm, d), lambda i: (i, 0)),
        out_shape=jax.ShapeDtypeStruct((t, d), F32),
        scratch_shapes=[pltpu.SMEM((2, TOP_K * tm // LANES, LANES), jnp.int32),
                        pltpu.VMEM((TOP_K, tm, dw), jnp.uint32),
                        pltpu.SemaphoreType.DMA((2,)),
                        pltpu.SemaphoreType.DMA(())],
        compiler_params=_params(("arbitrary",)),
        name="combine_final",
    )(dest_tiles, h2p, ys, wts_t, x1, mod3, pn, wg, wu, wd)


def kernel(x, c, positions, w_ada, b_ada, pre_norm_mix, post_norm_mix, w_in, conv_w, a_log, dt_bias,
           dn_norm_w, rt_norm_w, w_out, pre_norm_ffn, post_norm_ffn, w_router, router_bias,
           w_gate_exp, w_up_exp, w_down_exp, w_gate_sh, w_up_sh, w_down_sh):
    batch, seq, d = x.shape
    t = batch * seq
    depth = w_ada.shape[0]
    x2d = x.reshape(t, d)
    tm = min(MOE_TILE, seq)

    half = RT_DK // 2
    theta = 1.0 / (ROPE_BASE ** jnp.linspace(0.0, 1.0, half, dtype=F32))
    ang = positions.astype(F32).reshape(t, 1) * theta[None, :]
    cos = jnp.tile(jnp.cos(ang), (1, RT_HEADS))
    sin = jnp.tile(jnp.sin(ang), (1, RT_HEADS))

    c_pad = jnp.zeros((8, d), F32).at[:batch].set(c)
    hd = np.arange(RT_HEADS)[:, None] * RT_DK + np.arange(half)[None, :]
    rot_perm = np.concatenate([hd.reshape(-1), (hd + half).reshape(-1)])

    n_tiles = -(-t * TOP_K // tm) + N_EXPERTS
    n_tmap = -(-n_tiles // LANES) * LANES

    for l in range(depth):
        mod = _ada(c_pad, w_ada[l], b_ada[l].reshape(1, -1))
        mod3 = jnp.zeros((batch, 8, d), F32).at[:, :6].set(mod[:batch].reshape(batch, 6, d))

        wl = w_in[l]
        o = 0
        segs = []
        for wdt in (DN_WIDTH, DN_WIDTH, DN_WIDTH, DN_WIDTH, DN_HEADS, DN_HEADS, RT_QK, RT_QK, RT_WIDTH, RT_WIDTH):
            segs.append(wl[:, o:o + wdt])
            o += wdt
        dq, dk, dv, dz, da, db, rq, rk, rv, rg = segs
        w_main = jnp.concatenate([dq, dk, dv, dz, rq[:, rot_perm], rk[:, rot_perm], rv, rg], axis=1).astype(BF16)
        w_ab = (jnp.zeros((d, 2 * LANES), F32).at[:, :DN_HEADS].set(da)
                .at[:, LANES:LANES + DN_HEADS].set(db)).astype(BF16)

        proj, ab = _inproj(x2d, mod3, pre_norm_mix[l].reshape(1, d), w_main, w_ab, seq)

        alog = jnp.zeros((1, LANES), F32).at[0, :DN_HEADS].set(a_log[l])
        dtb = jnp.zeros((1, LANES), F32).at[0, :DN_HEADS].set(dt_bias[l])
        o_a = _deltanet(proj, ab, conv_w[l], alog, dtb, dn_norm_w[l].reshape(1, DN_DV), batch, seq)
        o_b = _retention(proj, cos, sin, rt_norm_w[l].reshape(1, RT_WIDTH), batch, seq)

        wo = w_out[l].astype(BF16)
        x1, h2p, logits_t = _outproj(o_a, o_b, wo[:DN_WIDTH], wo[DN_WIDTH:], x2d, mod3,
                                     post_norm_mix[l].reshape(1, d), pre_norm_ffn[l].reshape(1, d),
                                     w_router[l].T, seq)

        dest, wts, tmap = _route(logits_t, router_bias[l].reshape(N_EXPERTS, 1), n_tmap, tile=tm)
        dest_tiles = (dest.reshape(TOP_K, t // tm, tm).transpose(1, 0, 2)
                      .reshape(t // tm, TOP_K * tm // LANES, LANES))
        xs = _dispatch(tmap, dest_tiles, h2p, n_tiles, tm=tm)
        ys = _moe_experts(tmap[0, :n_tiles], tmap[1, :n_tiles], xs,
                          w_gate_exp[l], w_up_exp[l], w_down_exp[l], tm=tm)

        x2d = _final(dest_tiles, h2p, ys, wts.T, x1, mod3, post_norm_ffn[l].reshape(1, d),
                     w_gate_sh[l].astype(BF16), w_up_sh[l].astype(BF16), w_down_sh[l].astype(BF16), seq, tm=tm)
    return x2d.reshape(batch, seq, d)
```

```python
import functools
import math

import numpy as np
import jax
import jax.numpy as jnp
from jax import lax
from jax.experimental import pallas as pl
from jax.experimental.pallas import tpu as pltpu

F32 = jnp.float32
BF16 = jnp.bfloat16

D_MODEL = 2048
DN_HEADS = 8
DN_DK = 128
DN_DV = 128
DN_CONV = 4
DN_CHUNK = 64
RT_HEADS = 8
RT_DK = 64
RT_DV = 128
ROPE_BASE = 10000.0
DN_WIDTH = DN_HEADS * DN_DV
RT_WIDTH = RT_HEADS * RT_DV
RT_QK = RT_HEADS * RT_DK
N_EXPERTS = 64
TOP_K = 8
N_GROUPS = 8
TOPK_GROUPS = 4
EXPERT_FF = 512
SHARED_FF = 512
ROUTED_SCALE = 2.5
NORM_EPS = 1e-6

LANES = 128
VMEM_LIMIT = 56 * 1024 * 1024

PROJ_WIDTH = 3 * DN_WIDTH + DN_WIDTH + 2 * RT_QK + 2 * RT_WIDTH
DN_ROWS = 256
RT_CHUNK = 256
MOE_TILE = 256


def _silu(x):
    return x * jax.nn.sigmoid(x)


def _softplus(x):
    return jnp.maximum(x, 0.0) + jnp.log1p(jnp.exp(-jnp.abs(x)))


def _dot(a, b, precision=None):
    return jnp.dot(a, b, preferred_element_type=F32, precision=precision)


def _dot_nt(a, b, precision=None):
    return lax.dot_general(a, b, (((1,), (1,)), ((), ())), preferred_element_type=F32, precision=precision)


def _dot_tn(a, b, precision=None):
    return lax.dot_general(a, b, (((0,), (0,)), ((), ())), preferred_element_type=F32, precision=precision)


def _rms(x, w):
    return x * lax.rsqrt(jnp.mean(x * x, axis=-1, keepdims=True) + NORM_EPS) * w


def _params(sem):
    return pltpu.CompilerParams(dimension_semantics=sem, vmem_limit_bytes=VMEM_LIMIT)


def _ada_kernel(c_ref, w_ref, b_ref, o_ref):
    ca = _silu(c_ref[...])
    o_ref[...] = _dot(ca.astype(BF16), w_ref[...].astype(BF16)) + b_ref[...]


def _ada(c_pad, w_ada, b_ada, tn=1024):
    m, k = c_pad.shape
    n = w_ada.shape[1]
    return pl.pallas_call(
        _ada_kernel,
        grid=(n // tn,),
        in_specs=[pl.BlockSpec((m, k), lambda j: (0, 0)),
                  pl.BlockSpec((k, tn), lambda j: (0, j)),
                  pl.BlockSpec((1, tn), lambda j: (0, j))],
        out_specs=pl.BlockSpec((m, tn), lambda j: (0, j)),
        out_shape=jax.ShapeDtypeStruct((m, n), F32),
        compiler_params=_params(("arbitrary",)),
        name="ada_mod",
    )(c_pad, w_ada, b_ada)


def _inproj_kernel(x_ref, mod_ref, nw_ref, w_ref, wab_ref, o_ref, ab_ref, h_ref):
    j = pl.program_id(1)

    @pl.when(j == 0)
    def _():
        x = x_ref[...]
        h = _rms(x, nw_ref[...]) * (1.0 + mod_ref[1:2, :]) + mod_ref[0:1, :]
        hb = h.astype(BF16)
        h_ref[...] = hb
        ab_ref[...] = _dot(hb, wab_ref[...])

    o_ref[...] = _dot(h_ref[...], w_ref[...]).astype(o_ref.dtype)


def _inproj(x2d, mod3, nw, w_main, w_ab, seq, tm=1024, tn=1024):
    t, d = x2d.shape
    n = w_main.shape[1]
    tm = min(tm, seq)
    per_b = seq // tm
    return pl.pallas_call(
        _inproj_kernel,
        grid=(t // tm, n // tn),
        in_specs=[pl.BlockSpec((tm, d), lambda i, j: (i, 0)),
                  pl.BlockSpec((None, 8, d), lambda i, j: (i // per_b, 0, 0)),
                  pl.BlockSpec((1, d), lambda i, j: (0, 0)),
                  pl.BlockSpec((d, tn), lambda i, j: (0, j)),
                  pl.BlockSpec((d, 2 * LANES), lambda i, j: (0, 0))],
        out_specs=[pl.BlockSpec((tm, tn), lambda i, j: (i, j)),
                   pl.BlockSpec((tm, 2 * LANES), lambda i, j: (i, 0))],
        out_shape=[jax.ShapeDtypeStruct((t, n), BF16),
                   jax.ShapeDtypeStruct((t, 2 * LANES), F32)],
        scratch_shapes=[pltpu.VMEM((tm, d), BF16)],
        compiler_params=_params(("arbitrary", "arbitrary")),
        name="in_proj",
    )(x2d, mod3, nw, w_main, w_ab)


def _dn_kernel(qkv_ref, z_ref, ab_ref, cw_ref, alog_ref, dtb_ref, nw_ref, o_ref,
               ext_ref, q_s, k_s, v_s, g_s, b_s, st_ref, *, rows, chunk):
    n = pl.program_id(1)
    w3 = 3 * DN_WIDTH

    @pl.when(n == 0)
    def _():
        ext_ref[0:8, :] = jnp.zeros((8, w3), F32)
        st_ref[...] = jnp.zeros_like(st_ref)

    x = qkv_ref[...].astype(F32)
    ext_ref[8:8 + rows, :] = x
    cw = cw_ref[...]
    acc = x * cw[3:4, :]
    for j in range(DN_CONV - 1):
        acc = acc + ext_ref[5 + j:5 + j + rows, :] * cw[j:j + 1, :]
    ext_ref[0:8, :] = x[rows - 8:rows, :]
    act = _silu(acc)

    scale = DN_DK ** -0.5
    for h in range(DN_HEADS):
        sl = slice(h * DN_DK, (h + 1) * DN_DK)
        qh = act[:, sl]
        kh = act[:, DN_WIDTH + h * DN_DK:DN_WIDTH + (h + 1) * DN_DK]
        q_s[:, sl] = qh * (lax.rsqrt(jnp.sum(qh * qh, axis=-1, keepdims=True) + NORM_EPS) * scale)
        k_s[:, sl] = kh * lax.rsqrt(jnp.sum(kh * kh, axis=-1, keepdims=True) + NORM_EPS)
    v_s[...] = act[:, 2 * DN_WIDTH:]

    ab = ab_ref[...]
    g_s[...] = -jnp.exp(alog_ref[...]) * _softplus(ab[:, :LANES] + dtb_ref[...])
    b_s[...] = jax.nn.sigmoid(ab[:, LANES:])

    c = chunk
    ri = lax.broadcasted_iota(jnp.int32, (c, c), 0)
    ci = lax.broadcasted_iota(jnp.int32, (c, c), 1)
    causal = ri >= ci
    strict = ri > ci
    ltri = causal.astype(F32)
    utri = (ri <= ci).astype(F32)
    eye = (ri == ci).astype(F32)
    nw = nw_ref[...]
    hp = lax.Precision.HIGHEST

    def chunk_body(ic, carry):
        r0 = pl.multiple_of(ic * c, c)
        g_c = g_s[pl.ds(r0, c), :]
        b_c = b_s[pl.ds(r0, c), :]
        gc = _dot(ltri, g_c, hp)
        gct = _dot_tn(g_c, utri, hp)
        glast = gc[c - 1:c, :]
        eg = jnp.exp(gc)
        ekd = jnp.exp(glast - gc)
        cdv = jnp.exp(glast)
        hs = range(DN_HEADS)
        sls = [slice(h * DN_DK, (h + 1) * DN_DK) for h in hs]
        q = [q_s[pl.ds(r0, c), sls[h]] for h in hs]
        k = [k_s[pl.ds(r0, c), sls[h]] for h in hs]
        v = [v_s[pl.ds(r0, c), sls[h]] for h in hs]
        decay = [jnp.where(causal, jnp.exp(jnp.where(causal, gc[:, h:h + 1] - gct[h:h + 1, :], 0.0)), 0.0)
                 for h in hs]
        bcol = [b_c[:, h:h + 1] for h in hs]
        kb = [k[h] * bcol[h] for h in hs]
        a_mat = [jnp.where(strict, _dot_nt(kb[h], k[h]) * decay[h], 0.0) for h in hs]
        attn = [jnp.where(causal, _dot_nt(q[h], k[h]) * decay[h], 0.0) for h in hs]
        xi = [eye - a_mat[h] for h in hs]
        pw = [_dot(a_mat[h], a_mat[h]) for h in hs]
        kk = 2
        while kk < c:
            xi = [xi[h] + _dot(xi[h], pw[h]) for h in hs]
            kk *= 2
            if kk < c:
                pw = [_dot(pw[h], pw[h]) for h in hs]
        rhs = [jnp.concatenate([kb[h] * eg[:, h:h + 1], v[h] * bcol[h]], axis=1) for h in hs]
        sol = [_dot(xi[h], rhs[h]) for h in hs]
        s = [st_ref[h] for h in hs]
        v_new = [sol[h][:, DN_DK:] - _dot(sol[h][:, :DN_DK], s[h]) for h in hs]
        out = [_dot(q[h] * eg[:, h:h + 1], s[h]) + _dot(attn[h], v_new[h]) for h in hs]
        for h in hs:
            st_ref[h] = s[h] * cdv[:, h:h + 1] + _dot_tn(k[h] * ekd[:, h:h + 1], v_new[h])
        for h in hs:
            zz = z_ref[pl.ds(r0, c), sls[h]].astype(F32)
            o_ref[pl.ds(r0, c), sls[h]] = (_rms(out[h], nw) * _silu(zz)).astype(o_ref.dtype)
        return carry

    lax.fori_loop(0, rows // c, chunk_body, 0)


def _deltanet(proj, ab, conv_w, alog, dtb, nw, batch, seq, rows=DN_ROWS, chunk=DN_CHUNK):
    rows = min(rows, seq)
    nb = seq // rows
    t = batch * seq
    w3 = 3 * DN_WIDTH
    kern = functools.partial(_dn_kernel, rows=rows, chunk=chunk)
    return pl.pallas_call(
        kern,
        grid=(batch, nb),
        in_specs=[pl.BlockSpec((rows, w3), lambda b, n: (b * nb + n, 0)),
                  pl.BlockSpec((rows, DN_WIDTH), lambda b, n: (b * nb + n, 3)),
                  pl.BlockSpec((rows, 2 * LANES), lambda b, n: (b * nb + n, 0)),
                  pl.BlockSpec((DN_CONV, w3), lambda b, n: (0, 0)),
                  pl.BlockSpec((1, LANES), lambda b, n: (0, 0)),
                  pl.BlockSpec((1, LANES), lambda b, n: (0, 0)),
                  pl.BlockSpec((1, DN_DV), lambda b, n: (0, 0))],
        out_specs=pl.BlockSpec((rows, DN_WIDTH), lambda b, n: (b * nb + n, 0)),
        out_shape=jax.ShapeDtypeStruct((t, DN_WIDTH), BF16),
        scratch_shapes=[pltpu.VMEM((rows + 8, w3), F32),
                        pltpu.VMEM((rows, DN_WIDTH), F32),
                        pltpu.VMEM((rows, DN_WIDTH), F32),
                        pltpu.VMEM((rows, DN_WIDTH), F32),
                        pltpu.VMEM((rows, LANES), F32),
                        pltpu.VMEM((rows, LANES), F32),
                        pltpu.VMEM((DN_HEADS, DN_DK, DN_DV), F32)],
        compiler_params=_params(("arbitrary", "arbitrary")),
        name="deltanet",
    )(proj, proj, ab, conv_w, alog, dtb, nw)


def _rt_log_gamma(idx):
    return jnp.log(1.0 - jnp.exp2(-5.0 - idx.astype(F32)))


def _rt_kernel(qk_ref, v_ref, g_ref, cos_ref, sin_ref, nw_ref, o_ref, st_ref, msk_ref, *, chunk):
    n = pl.program_id(1)
    c = chunk
    half = RT_QK // 2

    @pl.when(n == 0)
    def _():
        st_ref[...] = jnp.zeros_like(st_ref)
        rh = (lax.broadcasted_iota(jnp.int32, (RT_QK, RT_WIDTH), 0) & (half - 1)) >> 5
        ch = lax.broadcasted_iota(jnp.int32, (RT_QK, RT_WIDTH), 1) >> 7
        msk_ref[...] = jnp.where(rh == ch, jnp.exp(float(c) * _rt_log_gamma(rh)), 0.0)

    cos = cos_ref[...]
    sin = sin_ref[...]
    qk = qk_ref[...].astype(F32)
    q1, q2 = qk[:, 0:half], qk[:, half:2 * half]
    k1, k2 = qk[:, 2 * half:3 * half], qk[:, 3 * half:4 * half]
    kscale = RT_DK ** -0.5
    q = jnp.concatenate([q1 * cos - q2 * sin, q2 * cos + q1 * sin], axis=1)
    k = jnp.concatenate([k1 * cos - k2 * sin, k2 * cos + k1 * sin], axis=1) * kscale
    v = v_ref[...].astype(F32)

    lane_h = (lax.broadcasted_iota(jnp.int32, (1, RT_QK), 1) & (half - 1)) >> 5
    lg_lane = _rt_log_gamma(lane_h)
    row = lax.broadcasted_iota(jnp.int32, (c, 1), 0).astype(F32)
    q_dec = q * jnp.exp((row + 1.0) * lg_lane)
    k_dec = k * jnp.exp((float(c) - 1.0 - row) * lg_lane)

    ri = lax.broadcasted_iota(jnp.int32, (c, c), 0)
    ci = lax.broadcasted_iota(jnp.int32, (c, c), 1)
    causal = ri >= ci
    rel = jnp.where(causal, (ri - ci).astype(F32), 0.0)

    s_all = st_ref[...]
    inter = _dot(q_dec, s_all)
    msk = msk_ref[...]
    st_ref[...] = s_all * msk + jnp.where(msk > 0.0, _dot_tn(k_dec, v), 0.0)

    nw = nw_ref[...]
    for h in range(RT_HEADS):
        sl = slice(h * RT_DV, (h + 1) * RT_DV)
        lg = math.log(1.0 - 2.0 ** (-5.0 - h))
        d_h = jnp.where(causal, jnp.exp(rel * lg), 0.0)
        qm = jnp.where(lane_h == h, q, 0.0)
        sc = _dot_nt(qm, k) * d_h
        o = _dot(sc, v[:, sl]) + inter[:, sl]
        mu = jnp.mean(o, axis=-1, keepdims=True)
        var = jnp.mean(jnp.square(o - mu), axis=-1, keepdims=True)
        y = (o - mu) * lax.rsqrt(var + NORM_EPS) * nw[:, sl]
        gg = g_ref[:, sl].astype(F32)
        o_ref[:, sl] = (y * _silu(gg)).astype(o_ref.dtype)


def _retention(proj, cos, sin, nw, batch, seq, chunk=RT_CHUNK):
    chunk = min(chunk, seq)
    nb = seq // chunk
    t = batch * seq
    kern = functools.partial(_rt_kernel, chunk=chunk)
    return pl.pallas_call(
        kern,
        grid=(batch, nb),
        in_specs=[pl.BlockSpec((chunk, 2 * RT_QK), lambda b, n: (b * nb + n, 4)),
                  pl.BlockSpec((chunk, RT_WIDTH), lambda b, n: (b * nb + n, 5)),
                  pl.BlockSpec((chunk, RT_WIDTH), lambda b, n: (b * nb + n, 6)),
                  pl.BlockSpec((chunk, RT_QK // 2), lambda b, n: (b * nb + n, 0)),
                  pl.BlockSpec((chunk, RT_QK // 2), lambda b, n: (b * nb + n, 0)),
                  pl.BlockSpec((1, RT_WIDTH), lambda b, n: (0, 0))],
        out_specs=pl.BlockSpec((chunk, RT_WIDTH), lambda b, n: (b * nb + n, 0)),
        out_shape=jax.ShapeDtypeStruct((t, RT_WIDTH), BF16),
        scratch_shapes=[pltpu.VMEM((RT_QK, RT_WIDTH), F32),
                        pltpu.VMEM((RT_QK, RT_WIDTH), F32)],
        compiler_params=_params(("arbitrary", "arbitrary")),
        name="retention",
    )(proj, proj, proj, cos, sin, nw)


def _pack_pair(a, b):
    au = lax.bitcast_convert_type(a.astype(BF16).astype(F32), jnp.uint32)
    bu = lax.bitcast_convert_type(b.astype(BF16).astype(F32), jnp.uint32)
    return (au >> 16) | bu


def _unpack_pair(w):
    a = lax.bitcast_convert_type(w << 16, F32)
    b = lax.bitcast_convert_type(w & jnp.uint32(0xFFFF0000), F32)
    return a, b


def _outproj_kernel(oa_ref, ob_ref, wa_ref, wb_ref, x_ref, mod_ref, pn_ref, fn_ref, wr_ref,
                    x1_ref, h2_ref, lg_ref):
    y = _dot(oa_ref[...], wa_ref[...]) + _dot(ob_ref[...], wb_ref[...])
    x1 = x_ref[...] + mod_ref[2:3, :] * _rms(y, pn_ref[...])
    x1_ref[...] = x1
    h2 = _rms(x1, fn_ref[...]) * (1.0 + mod_ref[4:5, :]) + mod_ref[3:4, :]
    hw = h2.shape[1] // 2
    h2_ref[...] = _pack_pair(h2[:, :hw], h2[:, hw:])
    h_hi = h2.astype(BF16)
    h_lo = (h2 - h_hi.astype(F32)).astype(BF16)
    wr = wr_ref[...]
    w_hi = wr.astype(BF16)
    w_lo = (wr - w_hi.astype(F32)).astype(BF16)
    lg_ref[...] = _dot_nt(w_hi, h_hi) + (_dot_nt(w_lo, h_hi) + _dot_nt(w_hi, h_lo))


def _outproj(o_a, o_b, wa, wb, x2d, mod3, pn, fn, wr_t, seq, tm=512):
    t, d = x2d.shape
    tm = min(tm, seq)
    per_b = seq // tm
    return pl.pallas_call(
        _outproj_kernel,
        grid=(t // tm,),
        in_specs=[pl.BlockSpec((tm, DN_WIDTH), lambda i: (i, 0)),
                  pl.BlockSpec((tm, RT_WIDTH), lambda i: (i, 0)),
                  pl.BlockSpec((DN_WIDTH, d), lambda i: (0, 0)),
                  pl.BlockSpec((RT_WIDTH, d), lambda i: (0, 0)),
                  pl.BlockSpec((tm, d), lambda i: (i, 0)),
                  pl.BlockSpec((None, 8, d), lambda i: (i // per_b, 0, 0)),
                  pl.BlockSpec((1, d), lambda i: (0, 0)),
                  pl.BlockSpec((1, d), lambda i: (0, 0)),
                  pl.BlockSpec((N_EXPERTS, d), lambda i: (0, 0))],
        out_specs=[pl.BlockSpec((tm, d), lambda i: (i, 0)),
                   pl.BlockSpec((tm, d // 2), lambda i: (i, 0)),
                   pl.BlockSpec((N_EXPERTS, tm), lambda i: (0, i))],
        out_shape=[jax.ShapeDtypeStruct((t, d), F32),
                   jax.ShapeDtypeStruct((t, d // 2), jnp.uint32),
                   jax.ShapeDtypeStruct((N_EXPERTS, t), F32)],
        compiler_params=_params(("arbitrary",)),
        name="out_proj",
    )(o_a, o_b, wa, wb, x2d, mod3, pn, fn, wr_t)


def _allreduce8(x, op):
    for sh in (1, 2, 4):
        x = op(x, pltpu.roll(x, sh, 0))
    return x


def _route_kernel(lg_ref, bias_ref, dest_ref, wts_ref, tmap_ref, cnt_s, base_s, ust_s, *, tn, tile):
    p = pl.program_id(0)
    j = pl.program_id(1)
    ng = N_GROUPS
    gs = N_EXPERTS // N_GROUPS
    n_tmap = tmap_ref.shape[1]
    hp = lax.Precision.HIGHEST

    @pl.when((p == 0) & (j == 0))
    def _():
        cnt_s[...] = jnp.zeros_like(cnt_s)
        base_s[...] = jnp.zeros_like(base_s)
        tmap_ref[...] = jnp.zeros_like(tmap_ref)
        r = lax.broadcasted_iota(jnp.int32, (tn, tn), 0)
        c = lax.broadcasted_iota(jnp.int32, (tn, tn), 1)
        ust_s[...] = jnp.where(r < c, 1.0, 0.0).astype(BF16)

    @pl.when((p == 1) & (j == 0))
    def _():
        cnt = cnt_s[...]
        padded = jnp.floor((cnt + float(tile - 1)) * (1.0 / tile)) * float(tile)
        ri = lax.broadcasted_iota(jnp.int32, (N_EXPERTS, N_EXPERTS), 0)
        ci = lax.broadcasted_iota(jnp.int32, (N_EXPERTS, N_EXPERTS), 1)
        pad_end = _dot(jnp.where(ri >= ci, 1.0, 0.0), padded, hp)
        base_s[...] = pad_end - padded
        cnt_s[...] = jnp.zeros_like(cnt_s)
        ts = lax.broadcasted_iota(jnp.int32, (N_EXPERTS, n_tmap), 1).astype(F32) * float(tile)
        ended = jnp.where(pad_end[:, 0:1] <= ts, 1.0, 0.0)
        te = jnp.minimum(jnp.sum(ended, axis=0, keepdims=True), float(N_EXPERTS - 1))
        total = pad_end[N_EXPERTS - 1:N_EXPERTS, 0:1]
        tv = jnp.where(ts[0:1, :] < total, 1, 0)
        diag = (lax.broadcasted_iota(jnp.int32, (N_EXPERTS, LANES), 0)
                == lax.broadcasted_iota(jnp.int32, (N_EXPERTS, LANES), 1))
        zpad = jnp.zeros((1, n_tmap - LANES), F32)
        cnt_row = jnp.concatenate([jnp.sum(jnp.where(diag, cnt, 0.0), axis=0, keepdims=True), zpad], axis=1)
        ps_row = jnp.concatenate([jnp.sum(jnp.where(diag, pad_end - padded, 0.0), axis=0, keepdims=True), zpad],
                                 axis=1)
        row = lax.broadcasted_iota(jnp.int32, (8, n_tmap), 0)
        tmap_ref[...] = jnp.where(row == 0, te.astype(jnp.int32),
                                  jnp.where(row == 1, tv,
                                            jnp.where(row == 2, cnt_row.astype(jnp.int32),
                                                      jnp.where(row == 3, ps_row.astype(jnp.int32), 0))))

    s_all = jax.nn.sigmoid(lg_ref[...])
    sel_all = s_all + bias_ref[...]
    sub = lax.broadcasted_iota(jnp.int32, (gs, tn), 0)
    neg = -jnp.inf

    vs, ss, gsc = [], [], []
    for g in range(ng):
        v = sel_all[g * gs:(g + 1) * gs, :]
        m1 = _allreduce8(v, jnp.maximum)
        first = _allreduce8(jnp.where(v == m1, sub, gs), jnp.minimum)
        m2 = _allreduce8(jnp.where(sub == first, neg, v), jnp.maximum)
        vs.append(v)
        ss.append(s_all[g * gs:(g + 1) * gs, :])
        gsc.append(m1 + m2)
    vm = []
    for g in range(ng):
        ahead = jnp.zeros((gs, tn), jnp.int32)
        for g2 in range(ng):
            if g2 == g:
                continue
            b = (gsc[g2] >= gsc[g]) if g2 < g else (gsc[g2] > gsc[g])
            ahead = ahead + jnp.where(b, 1, 0)
        vm.append(jnp.where(ahead < TOPK_GROUPS, vs[g], neg))
    firsts = []
    mem = [jnp.zeros((gs, tn), F32) for _ in range(ng)]
    w_rows = jnp.zeros((gs, tn), F32)
    for k in range(TOP_K):
        mx = vm[0]
        for g in range(1, ng):
            mx = jnp.maximum(mx, vm[g])
        mx = _allreduce8(mx, jnp.maximum)
        cand = jnp.full((gs, tn), N_EXPERTS, jnp.int32)
        for g in range(ng):
            cand = jnp.minimum(cand, jnp.where(vm[g] == mx, sub + g * gs, N_EXPERTS))
        first = _allreduce8(cand, jnp.minimum)
        wk = jnp.zeros((gs, tn), F32)
        for g in range(ng):
            hit = (sub + g * gs) == first
            vm[g] = jnp.where(hit, neg, vm[g])
            wk = wk + jnp.where(hit, ss[g], 0.0)
            mem[g] = mem[g] + jnp.where(hit, 1.0, 0.0)
        wk = _allreduce8(wk, jnp.add)
        w_rows = jnp.where(sub == k, wk, w_rows)
        firsts.append(first)
    wsum = _allreduce8(w_rows, jnp.add)
    wts_ref[...] = w_rows / wsum * ROUTED_SCALE

    m_all = jnp.concatenate(mem, axis=0)
    pref = _dot(m_all.astype(BF16), ust_s[...])
    tot = pref + (base_s[...] + cnt_s[...])[:, 0:1]
    d_rows = jnp.zeros((gs, tn), F32)
    for k in range(TOP_K):
        dk = jnp.zeros((gs, tn), F32)
        for g in range(ng):
            dk = dk + jnp.where((sub + g * gs) == firsts[k], tot[g * gs:(g + 1) * gs, :], 0.0)
        d_rows = jnp.where(sub == k, _allreduce8(dk, jnp.add), d_rows)
    dest_ref[...] = d_rows.astype(jnp.int32)
    cnt_s[...] = cnt_s[...] + jnp.sum(m_all, axis=1, keepdims=True)


def _route(logits_t, bias_col, n_tmap, tn=512, tile=MOE_TILE):
    ne, t = logits_t.shape
    tn = min(tn, t)
    nb = t // tn
    kern = functools.partial(_route_kernel, tn=tn, tile=tile)
    return pl.pallas_call(
        kern,
        grid=(2, nb),
        in_specs=[pl.BlockSpec((ne, tn), lambda p, j: (0, j)),
                  pl.BlockSpec((ne, 1), lambda p, j: (0, 0))],
        out_specs=[pl.BlockSpec((TOP_K, tn), lambda p, j: (0, j * p)),
                   pl.BlockSpec((TOP_K, tn), lambda p, j: (0, j * p)),
                   pl.BlockSpec((8, n_tmap), lambda p, j: (0, 0))],
        out_shape=[jax.ShapeDtypeStruct((TOP_K, t), jnp.int32),
                   jax.ShapeDtypeStruct((TOP_K, t), F32),
                   jax.ShapeDtypeStruct((8, n_tmap), jnp.int32)],
        scratch_shapes=[pltpu.VMEM((ne, LANES), F32),
                        pltpu.VMEM((ne, LANES), F32),
                        pltpu.VMEM((tn, tn), BF16)],
        compiler_params=_params(("arbitrary", "arbitrary")),
        name="router",
    )(logits_t, bias_col)


def _idx_copy(idx_hbm, idx_s, sem, tile_i, slot, n_idx):
    return pltpu.make_async_copy(idx_hbm.at[pl.ds(pl.multiple_of(tile_i * n_idx, n_idx), n_idx)],
                                 idx_s.at[pl.ds(pl.multiple_of(slot * n_idx, n_idx), n_idx)], sem.at[slot])


def _dispatch_kernel(tmap_ref, dest_hbm, h_ref, xs_hbm, idx_s, zbuf, isem, rsem, zsem, *, tm, n_tiles):
    i = pl.program_id(0)
    n = pl.num_programs(0)
    slot = i % 2
    sh = tm.bit_length() - 1

    def pad_copy(r):
        return pltpu.make_async_copy(zbuf.at[pl.ds(0, 1)], xs_hbm.at[pl.ds(r, 1)], zsem)

    def tail_copy(j):
        return pltpu.make_async_copy(zbuf, xs_hbm.at[pl.ds(pl.multiple_of(j * tm, tm), tm)], zsem)

    def pad_range(e):
        cnt = tmap_ref[2, e]
        ps = tmap_ref[3, e]
        return ps + cnt, ps + (((cnt + (tm - 1)) >> sh) << sh)

    def zero_fill(wait):
        def per_row(r, c):
            pad_copy(r).wait() if wait else pad_copy(r).start()
            return c

        def per_expert(e, c):
            lo, hi = pad_range(e)
            return lax.fori_loop(lo, hi, per_row, c)

        def per_tile(j, c):
            tail_copy(j).wait() if wait else tail_copy(j).start()
            return c

        lax.fori_loop(0, N_EXPERTS, per_expert, 0)
        lax.fori_loop(pad_range(N_EXPERTS - 1)[1] >> sh, n_tiles, per_tile, 0)

    @pl.when(i == 0)
    def _():
        zbuf[...] = jnp.zeros_like(zbuf)
        zero_fill(False)
        _idx_copy(dest_hbm, idx_s, isem, 0, 0, TOP_K * tm).start()

    _idx_copy(dest_hbm, idx_s, isem, i, slot, TOP_K * tm).wait()

    @pl.when(i + 1 < n)
    def _():
        _idx_copy(dest_hbm, idx_s, isem, i + 1, 1 - slot, TOP_K * tm).start()

    def row_copy(t, k):
        d = idx_s[slot * (TOP_K * tm) + k * tm + t]
        return pltpu.make_async_copy(h_ref.at[pl.ds(t, 1)], xs_hbm.at[pl.ds(d, 1)], rsem)

    def start_body(t, c):
        for k in range(TOP_K):
            row_copy(t, k).start(priority=k % 2)
        return c

    def wait_body(t, c):
        for k in range(TOP_K):
            row_copy(t, k).wait()
        return c

    lax.fori_loop(0, tm, start_body, 0)
    lax.fori_loop(0, tm, wait_body, 0)

    @pl.when(i == 0)
    def _():
        zero_fill(True)


def _dispatch(tmap, dest_tiles, h2p, n_tiles, tm=MOE_TILE):
    t, dw = h2p.shape
    nt = t // tm
    kern = functools.partial(_dispatch_kernel, tm=tm, n_tiles=n_tiles)
    grid_spec = pltpu.PrefetchScalarGridSpec(
        num_scalar_prefetch=1,
        grid=(nt,),
        in_specs=[pl.BlockSpec(memory_space=pl.ANY),
                  pl.BlockSpec((tm, dw), lambda i, tmap: (i, 0))],
        out_specs=pl.BlockSpec(memory_space=pl.ANY),
        scratch_shapes=[pltpu.SMEM((2 * TOP_K * tm,), jnp.int32),
                        pltpu.VMEM((tm, dw), jnp.uint32),
                        pltpu.SemaphoreType.DMA((2,)),
                        pltpu.SemaphoreType.DMA(()),
                        pltpu.SemaphoreType.DMA(())],
    )
    return pl.pallas_call(
        kern,
        grid_spec=grid_spec,
        out_shape=jax.ShapeDtypeStruct((n_tiles * tm, dw), jnp.uint32),
        compiler_params=_params(("arbitrary",)),
        name="dispatch",
    )(tmap, dest_tiles, h2p)


def _moe_kernel(te_ref, tv_ref, xs_ref, wg_ref, wu_ref, wd_ref, ys_ref, wgb, wub, wdb):
    i = pl.program_id(0)
    e = te_ref[i]
    prev = te_ref[jnp.maximum(i - 1, 0)]

    @pl.when((i == 0) | (e != prev))
    def _():
        wgb[...] = wg_ref[...].astype(BF16)
        wub[...] = wu_ref[...].astype(BF16)
        wdb[...] = wd_ref[...].astype(BF16)

    @pl.when(tv_ref[i] > 0)
    def _():
        xa, xb = _unpack_pair(xs_ref[...])
        x = jnp.concatenate([xa, xb], axis=1).astype(BF16)
        a = _silu(_dot(x, wgb[...])) * _dot(x, wub[...])
        y = _dot(a.astype(BF16), wdb[...])
        hw = y.shape[1] // 2
        ys_ref[...] = _pack_pair(y[:, :hw], y[:, hw:])

    @pl.when(tv_ref[i] == 0)
    def _():
        ys_ref[...] = jnp.zeros_like(ys_ref)


def _moe_experts(tile_e, tile_v, xs, wg, wu, wd, tm=MOE_TILE):
    n_rows, dw = xs.shape
    d = 2 * dw
    ff = wg.shape[-1]
    n_tiles = n_rows // tm
    grid_spec = pltpu.PrefetchScalarGridSpec(
        num_scalar_prefetch=2,
        grid=(n_tiles,),
        in_specs=[pl.BlockSpec((tm, dw), lambda i, te, tv: (i, 0)),
                  pl.BlockSpec((None, d, ff), lambda i, te, tv: (te[i], 0, 0)),
                  pl.BlockSpec((None, d, ff), lambda i, te, tv: (te[i], 0, 0)),
                  pl.BlockSpec((None, ff, d), lambda i, te, tv: (te[i], 0, 0))],
        out_specs=pl.BlockSpec((tm, dw), lambda i, te, tv: (i, 0)),
        scratch_shapes=[pltpu.VMEM((d, ff), BF16), pltpu.VMEM((d, ff), BF16), pltpu.VMEM((ff, d), BF16)],
    )
    return pl.pallas_call(
        _moe_kernel,
        grid_spec=grid_spec,
        out_shape=jax.ShapeDtypeStruct((n_rows, dw), jnp.uint32),
        compiler_params=_params(("arbitrary",)),
        name="moe_experts",
    )(tile_e, tile_v, xs, wg, wu, wd)


def _final_kernel(dest_hbm, h2_ref, ys_hbm, wt_ref, x1_ref, mod_ref, pn_ref, wg_ref, wu_ref, wd_ref,
                  o_ref, idx_s, buf, isem, rsem, *, tm):
    i = pl.program_id(0)
    n = pl.num_programs(0)
    slot = i % 2

    @pl.when(i == 0)
    def _():
        _idx_copy(dest_hbm, idx_s, isem, 0, 0, TOP_K * tm).start()

    _idx_copy(dest_hbm, idx_s, isem, i, slot, TOP_K * tm).wait()

    @pl.when(i + 1 < n)
    def _():
        _idx_copy(dest_hbm, idx_s, isem, i + 1, 1 - slot, TOP_K * tm).start()

    def row_copy(t, k):
        d = idx_s[slot * (TOP_K * tm) + k * tm + t]
        return pltpu.make_async_copy(ys_hbm.at[pl.ds(d, 1)], buf.at[k, pl.ds(t, 1)], rsem)

    def start_body(t, c):
        for k in range(TOP_K):
            row_copy(t, k).start(priority=k % 2)
        return c

    def wait_body(t, c):
        for k in range(TOP_K):
            row_copy(t, k).wait()
        return c

    lax.fori_loop(0, tm, start_body, 0)

    ha, hb = _unpack_pair(h2_ref[...])
    h = jnp.concatenate([ha, hb], axis=1).astype(BF16)
    a = _silu(_dot(h, wg_ref[...])) * _dot(h, wu_ref[...])
    y = _dot(a.astype(BF16), wd_ref[...])

    lax.fori_loop(0, tm, wait_body, 0)

    hw = y.shape[1] // 2
    ya = y[:, :hw]
    yb = y[:, hw:]
    wt = wt_ref[...]
    for k in range(TOP_K):
        ra, rb = _unpack_pair(buf[k])
        wk = wt[:, k:k + 1]
        ya = ya + wk * ra
        yb = yb + wk * rb
    y = jnp.concatenate([ya, yb], axis=1)
    o_ref[...] = x1_ref[...] + mod_ref[5:6, :] * _rms(y, pn_ref[...])


def _final(dest_tiles, h2p, ys, wts_t, x1, mod3, pn, wg, wu, wd, seq, tm=MOE_TILE):
    t, d = x1.shape
    dw = d // 2
    ff = wg.shape[-1]
    tm = min(tm, seq)
    per_b = seq // tm
    kern = functools.partial(_final_kernel, tm=tm)
    return pl.pallas_call(
        kern,
        grid=(t // tm,),
        in_specs=[pl.BlockSpec(memory_space=pl.ANY),
                  pl.BlockSpec((tm, dw), lambda i: (i, 0)),
                  pl.BlockSpec(memory_space=pl.ANY),
                  pl.BlockSpec((tm, TOP_K), lambda i: (i, 0)),
                  pl.BlockSpec((tm, d), lambda i: (i, 0)),
                  pl.BlockSpec((None, 8, d), lambda i: (i // per_b, 0, 0)),
                  pl.BlockSpec((1, d), lambda i: (0, 0)),
                  pl.BlockSpec((d, ff), lambda i: (0, 0)),
                  pl.BlockSpec((d, ff), lambda i: (0, 0)),
                  pl.BlockSpec((ff, d), lambda i: (0, 0))],
        out_specs=pl.BlockSpec((tm, d), lambda i: (i, 0)),
        out_shape=jax.ShapeDtypeStruct((t, d), F32),
        scratch_shapes=[pltpu.SMEM((2 * TOP_K * tm,), jnp.int32),
                        pltpu.VMEM((TOP_K, tm, dw), jnp.uint32),
                        pltpu.SemaphoreType.DMA((2,)),
                        pltpu.SemaphoreType.DMA(())],
        compiler_params=_params(("arbitrary",)),
        name="combine_final",
    )(dest_tiles, h2p, ys, wts_t, x1, mod3, pn, wg, wu, wd)


def kernel(x, c, positions, w_ada, b_ada, pre_norm_mix, post_norm_mix, w_in, conv_w, a_log, dt_bias,
           dn_norm_w, rt_norm_w, w_out, pre_norm_ffn, post_norm_ffn, w_router, router_bias,
           w_gate_exp, w_up_exp, w_down_exp, w_gate_sh, w_up_sh, w_down_sh):
    batch, seq, d = x.shape
    t = batch * seq
    depth = w_ada.shape[0]
    x2d = x.reshape(t, d)
    tm = min(MOE_TILE, seq)

    half = RT_DK // 2
    theta = 1.0 / (ROPE_BASE ** jnp.linspace(0.0, 1.0, half, dtype=F32))
    ang = positions.astype(F32).reshape(t, 1) * theta[None, :]
    cos = jnp.tile(jnp.cos(ang), (1, RT_HEADS))
    sin = jnp.tile(jnp.sin(ang), (1, RT_HEADS))

    c_pad = jnp.zeros((8, d), F32).at[:batch].set(c)
    hd = np.arange(RT_HEADS)[:, None] * RT_DK + np.arange(half)[None, :]
    rot_perm = np.concatenate([hd.reshape(-1), (hd + half).reshape(-1)])

    n_tiles = -(-t * TOP_K // tm) + N_EXPERTS
    n_tmap = -(-n_tiles // LANES) * LANES

    for l in range(depth):
        mod = _ada(c_pad, w_ada[l], b_ada[l].reshape(1, -1))
        mod3 = jnp.zeros((batch, 8, d), F32).at[:, :6].set(mod[:batch].reshape(batch, 6, d))

        wl = w_in[l]
        o = 0
        segs = []
        for wdt in (DN_WIDTH, DN_WIDTH, DN_WIDTH, DN_WIDTH, DN_HEADS, DN_HEADS, RT_QK, RT_QK, RT_WIDTH, RT_WIDTH):
            segs.append(wl[:, o:o + wdt])
            o += wdt
        dq, dk, dv, dz, da, db, rq, rk, rv, rg = segs
        w_main = jnp.concatenate([dq, dk, dv, dz, rq[:, rot_perm], rk[:, rot_perm], rv, rg], axis=1).astype(BF16)
        w_ab = (jnp.zeros((d, 2 * LANES), F32).at[:, :DN_HEADS].set(da)
                .at[:, LANES:LANES + DN_HEADS].set(db)).astype(BF16)

        proj, ab = _inproj(x2d, mod3, pre_norm_mix[l].reshape(1, d), w_main, w_ab, seq)

        alog = jnp.zeros((1, LANES), F32).at[0, :DN_HEADS].set(a_log[l])
        dtb = jnp.zeros((1, LANES), F32).at[0, :DN_HEADS].set(dt_bias[l])
        o_a = _deltanet(proj, ab, conv_w[l], alog, dtb, dn_norm_w[l].reshape(1, DN_DV), batch, seq)
        o_b = _retention(proj, cos, sin, rt_norm_w[l].reshape(1, RT_WIDTH), batch, seq)

        wo = w_out[l].astype(BF16)
        x1, h2p, logits_t = _outproj(o_a, o_b, wo[:DN_WIDTH], wo[DN_WIDTH:], x2d, mod3,
                                     post_norm_mix[l].reshape(1, d), pre_norm_ffn[l].reshape(1, d),
                                     w_router[l].T, seq)

        dest, wts, tmap = _route(logits_t, router_bias[l].reshape(N_EXPERTS, 1), n_tmap, tile=tm)
        dest_tiles = dest.reshape(TOP_K, t // tm, tm).transpose(1, 0, 2).reshape(t * TOP_K)
        xs = _dispatch(tmap, dest_tiles, h2p, n_tiles, tm=tm)
        ys = _moe_experts(tmap[0, :n_tiles], tmap[1, :n_tiles], xs,
                          w_gate_exp[l], w_up_exp[l], w_down_exp[l], tm=tm)

        x2d = _final(dest_tiles, h2p, ys, wts.T, x1, mod3, post_norm_ffn[l].reshape(1, d),
                     w_gate_sh[l].astype(BF16), w_up_sh[l].astype(BF16), w_down_sh[l].astype(BF16), seq, tm=tm)
    return x2d.reshape(batch, seq, d)
```
